```python
import jax, jax.numpy as jnp
from jax import lax
import numpy as np

D_MODEL = 1024
BATCH = 4
SEQ = 4096
DEPTH = 4
DEC_BATCH = 128
DEC_SEQ = 8
PAST_LEN = 8192
PAGE_SIZE = 128

HEAD_DIM = 64
H_A = 8
A_PATTERNS = ((128, 1), (512, 4), (2048, 16))
A_WINDOW_MAX = 2048
H_B = 8
KV_B = 2
G_B = H_B // KV_B
WINDOW_B = 128
BLOCK = 128
N_GROUPS = 4
EXPERTS_PER_GROUP = 4
N_EXPERTS = N_GROUPS * EXPERTS_PER_GROUP
TOP_K_IN_GROUP = 2
D_EXPERT = 512
ROPE_THETA = 10000.0
LN_EPS = 1e-5
NEG_INF = -1e30
SCALE = HEAD_DIM ** -0.5
DN_ALPHA = (2 * DEPTH) ** 0.25
DN_BETA = (8 * DEPTH) ** -0.25
D_QA = H_A * HEAD_DIM
D_QB = H_B * HEAD_DIM
D_KVB = KV_B * HEAD_DIM
D_MIX = D_QA + D_QB
D_IN = 3 * D_QA + D_QB + 2 * D_KVB
IN_SPLITS = (D_QA, 2 * D_QA, 3 * D_QA, 3 * D_QA + D_QB, 3 * D_QA + D_QB + D_KVB)

kernel_name = "hymba_dilated_swa_sink_hiermoe_deepnorm_step"


def layer_norm(x, g, b):
    xf = x.astype(jnp.float32)
    mu = jnp.mean(xf, axis=-1, keepdims=True)
    var = jnp.mean(jnp.square(xf - mu), axis=-1, keepdims=True)
    return ((xf - mu) * lax.rsqrt(var + LN_EPS) * g + b).astype(x.dtype)


def rope(x, pos):
    half = HEAD_DIM // 2
    inv_freq = ROPE_THETA ** (-jnp.arange(half, dtype=jnp.float32) / half)
    ang = pos.astype(jnp.float32)[:, None] * inv_freq[None, :]
    cos = jnp.cos(ang)[None, :, None, :].astype(x.dtype)
    sin = jnp.sin(ang)[None, :, None, :].astype(x.dtype)
    x1, x2 = x[..., :half], x[..., half:]
    return jnp.concatenate([x1 * cos - x2 * sin, x2 * cos + x1 * sin], axis=-1)


def masked_softmax_parts(s, valid, sink):
    s = jnp.where(valid, s, NEG_INF)
    m = jnp.max(s, axis=-1)
    if sink is not None:
        sink = sink.astype(jnp.float32)
        m = jnp.maximum(m, sink)
    p = jnp.exp(s - m[..., None])
    den = jnp.sum(p, axis=-1)
    if sink is not None:
        den = den + jnp.exp(sink - m)
    return p, m, den


def banded_attention(q, k, v, max_dist, sink=None):
    n, L, hq, hd = q.shape
    hkv = k.shape[2]
    g = hq // hkv
    nb = L // BLOCK
    qb = q.reshape(n, nb, BLOCK, hkv, g, hd)

    def with_prev(a):
        a = a.reshape(n, nb, BLOCK, hkv, hd)
        prev = jnp.concatenate([jnp.zeros_like(a[:, :1]), a[:, :-1]], axis=1)
        return jnp.concatenate([prev, a], axis=2)

    kk, vv = with_prev(k), with_prev(v)
    s = jnp.einsum("nbqhgd,nbkhd->nbhgqk", qb, kk).astype(jnp.float32) * SCALE
    q_off = jnp.arange(BLOCK)[:, None]
    k_off = jnp.arange(2 * BLOCK)[None, :] - BLOCK
    dist = q_off - k_off
    k_pos = jnp.arange(nb)[:, None, None] * BLOCK + k_off[None]
    valid = (dist >= 0) & (dist <= max_dist) & (k_pos >= 0)
    valid = valid[None, :, None, None]
    p, m, den = masked_softmax_parts(s, valid, None if sink is None else sink[..., None])
    o = jnp.einsum("nbhgqk,nbkhd->nbqhgd", p, vv.astype(jnp.float32))
    o = o / jnp.moveaxis(den, -1, 2)[..., None]
    lse = jnp.moveaxis(m + jnp.log(den), -1, 2)
    return o.reshape(n, L, hq, hd).astype(q.dtype), lse.reshape(n, L, hq)


def merge_by_denominators(outs, lses):
    w = jax.nn.softmax(jnp.stack(lses), axis=0)
    return jnp.einsum("pnth,pnthd->nthd", w, jnp.stack(outs).astype(jnp.float32))


def strided_split(a, dil, l_pad):
    n, T, h, d = a.shape
    L = T // dil
    a = a.reshape(n, L, dil, h, d).transpose(0, 2, 1, 3, 4).reshape(n * dil, L, h, d)
    return jnp.pad(a, ((0, 0), (0, l_pad - L), (0, 0), (0, 0)))


def dilated_attention_prompt(q, k, v):
    n, T = q.shape[:2]
    outs, lses = [], []
    for window, dil in A_PATTERNS:
        L = T // dil
        l_pad = -(-L // BLOCK) * BLOCK
        o, lse = banded_attention(strided_split(q, dil, l_pad), strided_split(k, dil, l_pad),
                                  strided_split(v, dil, l_pad), window // dil)
        o = o[:, :L].reshape(n, dil, L, H_A, HEAD_DIM).transpose(0, 2, 1, 3, 4).reshape(n, T, H_A, HEAD_DIM)
        lse = lse[:, :L].reshape(n, dil, L, H_A).transpose(0, 2, 1, 3).reshape(n, T, H_A)
        outs.append(o)
        lses.append(lse)
    return merge_by_denominators(outs, lses).astype(q.dtype)


def dilated_attention_sample(q, k_all, v_all):
    n, S = q.shape[:2]
    w_buf = k_all.shape[1] - S
    outs, lses = [], []
    for window, dil in A_PATTERNS:
        n_keys = window // dil + 1
        idx = w_buf + jnp.arange(S)[:, None] - dil * jnp.arange(n_keys)[None, :]
        valid = idx >= 0
        idx = jnp.maximum(idx, 0)
        kg = jnp.take(k_all, idx, axis=1)
        vg = jnp.take(v_all, idx, axis=1)
        s = jnp.einsum("nshd,nsjhd->nhsj", q, kg).astype(jnp.float32) * SCALE
        p, m, den = masked_softmax_parts(s, valid, None)
        o = jnp.einsum("nhsj,nsjhd->nshd", p, vg.astype(jnp.float32)) / jnp.moveaxis(den, 1, 2)[..., None]
        outs.append(o)
        lses.append(jnp.moveaxis(m + jnp.log(den), 1, 2))
    return merge_by_denominators(outs, lses).astype(q.dtype)


def swa_sink_attention_sample(q, k_all, v_all, sink):
    n, S = q.shape[:2]
    w_buf = k_all.shape[1] - S
    qg = q.reshape(n, S, KV_B, G_B, HEAD_DIM)
    s = jnp.einsum("nshgd,nkhd->nhgsk", qg, k_all).astype(jnp.float32) * SCALE
    dist = (w_buf + jnp.arange(S))[:, None] - jnp.arange(w_buf + S)[None, :]
    valid = (dist >= 0) & (dist < WINDOW_B)
    p, m, den = masked_softmax_parts(s, valid, sink[..., None])
    o = jnp.einsum("nhgsk,nkhd->nshgd", p, v_all.astype(jnp.float32)) / jnp.moveaxis(den, -1, 1)[..., None]
    return o.reshape(n, S, H_B, HEAD_DIM).astype(q.dtype)


def adaln_modulation(c, w_ada_l, b_ada_l):
    mod = (jax.nn.silu(c) @ w_ada_l + b_ada_l).reshape(c.shape[0], 6, 1, D_MODEL)
    return [mod[:, i] for i in range(6)]


def project_in(h, w_in_l, pos):
    n, t, _ = h.shape
    q_a, k_a, v_a, q_b, k_b, v_b = jnp.split(h @ w_in_l, IN_SPLITS, axis=-1)
    q_a = rope(q_a.reshape(n, t, H_A, HEAD_DIM), pos)
    k_a = rope(k_a.reshape(n, t, H_A, HEAD_DIM), pos)
    v_a = v_a.reshape(n, t, H_A, HEAD_DIM)
    q_b = rope(q_b.reshape(n, t, H_B, HEAD_DIM), pos)
    k_b = rope(k_b.reshape(n, t, KV_B, HEAD_DIM), pos)
    v_b = v_b.reshape(n, t, KV_B, HEAD_DIM)
    return q_a, k_a, v_a, q_b, k_b, v_b


def project_out(o_a, o_b, w_out_l):
    n, t = o_a.shape[:2]
    return jnp.concatenate([o_a.reshape(n, t, D_QA), o_b.reshape(n, t, D_QB)], axis=-1) @ w_out_l


def hierarchical_moe(h, w_rg, b_rg, w_re, b_re, w_gate, w_up, w_down):
    n, t, d = h.shape
    x = h.reshape(n * t, d)
    logit_g = (x @ w_rg + b_rg).astype(jnp.float32)
    g_sel = jnp.argmax(logit_g, axis=-1)
    p_group = jnp.max(jax.nn.softmax(logit_g, axis=-1), axis=-1, keepdims=True)
    logit_e = (x @ w_re + b_re).astype(jnp.float32).reshape(-1, N_GROUPS, EXPERTS_PER_GROUP)
    logit_e = jnp.einsum("ng,nge->ne", jax.nn.one_hot(g_sel, N_GROUPS, dtype=jnp.float32), logit_e)
    top_val, top_idx = lax.top_k(logit_e, TOP_K_IN_GROUP)
    w_pair = jax.nn.softmax(top_val, axis=-1) * p_group
    expert_id = g_sel[:, None] * EXPERTS_PER_GROUP + top_idx
    gates = jnp.einsum("nk,nke->ne", w_pair,
                       jax.nn.one_hot(expert_id, N_EXPERTS, dtype=jnp.float32)).astype(x.dtype)
    y = jnp.zeros_like(x)
    for e in range(N_EXPERTS):
        a = jax.nn.silu(x @ w_gate[e]) * (x @ w_up[e])
        y = y + gates[:, e:e + 1] * (a @ w_down[e])
    return y.reshape(n, t, d)


def setup_inputs(seed: int = 0) -> dict:
    key = jax.random.key(seed)
    ks = jax.random.split(key, 24)
    nrm = jax.random.normal
    f32 = jnp.float32
    w_a_buf = min(A_WINDOW_MAX, PAST_LEN)
    w_b_buf = min(WINDOW_B - 1, PAST_LEN)
    v_scale = jnp.concatenate([jnp.ones((2 * D_QA,), f32), jnp.full((D_QA,), DN_BETA, f32),
                               jnp.ones((D_QB + D_KVB,), f32), jnp.full((D_KVB,), DN_BETA, f32)])
    return {
        "x_prompt": nrm(ks[0], (BATCH, SEQ, D_MODEL), f32),
        "x_sample": nrm(ks[1], (DEC_BATCH, DEC_SEQ, D_MODEL), f32),
        "cache_a_k": nrm(ks[2], (DEPTH, DEC_BATCH, w_a_buf, H_A, HEAD_DIM), f32),
        "cache_a_v": nrm(ks[3], (DEPTH, DEC_BATCH, w_a_buf, H_A, HEAD_DIM), f32) * DN_BETA,
        "cache_b_k": nrm(ks[4], (DEPTH, DEC_BATCH, w_b_buf, KV_B, HEAD_DIM), f32),
        "cache_b_v": nrm(ks[5], (DEPTH, DEC_BATCH, w_b_buf, KV_B, HEAD_DIM), f32) * DN_BETA,
        "c_prompt": nrm(ks[6], (BATCH, D_MODEL), f32),
        "c_sample": nrm(ks[7], (DEC_BATCH, D_MODEL), f32),
        "w_in": nrm(ks[8], (DEPTH, D_MODEL, D_IN), f32) * D_MODEL ** -0.5 * v_scale,
        "w_out": nrm(ks[9], (DEPTH, D_MIX, D_MODEL), f32) * D_MIX ** -0.5 * DN_BETA,
        "attn_sinks": nrm(ks[10], (DEPTH, H_B), f32),
        "w_ada": nrm(ks[11], (DEPTH, D_MODEL, 6 * D_MODEL), f32) * 0.5 * D_MODEL ** -0.5,
        "b_ada": nrm(ks[12], (DEPTH, 6 * D_MODEL), f32) * 0.02,
        "ln1_g": 1.0 + 0.05 * nrm(ks[13], (DEPTH, D_MODEL), f32),
        "ln1_b": 0.02 * nrm(ks[14], (DEPTH, D_MODEL), f32),
        "ln2_g": 1.0 + 0.05 * nrm(ks[15], (DEPTH, D_MODEL), f32),
        "ln2_b": 0.02 * nrm(ks[16], (DEPTH, D_MODEL), f32),
        "w_router_group": nrm(ks[17], (DEPTH, D_MODEL, N_GROUPS), f32) * D_MODEL ** -0.5,
        "b_router_group": 0.01 * nrm(ks[18], (DEPTH, N_GROUPS), f32),
        "w_router_expert": nrm(ks[19], (DEPTH, D_MODEL, N_EXPERTS), f32) * D_MODEL ** -0.5,
        "b_router_expert": 0.01 * nrm(ks[20], (DEPTH, N_EXPERTS), f32),
        "w_gate": nrm(ks[21], (DEPTH, N_EXPERTS, D_MODEL, D_EXPERT), f32) * D_MODEL ** -0.5,
        "w_up": nrm(ks[22], (DEPTH, N_EXPERTS, D_MODEL, D_EXPERT), f32) * D_MODEL ** -0.5,
        "w_down": nrm(ks[23], (DEPTH, N_EXPERTS, D_EXPERT, D_MODEL), f32) * D_EXPERT ** -0.5 * DN_BETA,
    }


def reference(x_prompt, x_sample, cache_a_k, cache_a_v, cache_b_k, cache_b_v, c_prompt, c_sample,
              w_in, w_out, attn_sinks, w_ada, b_ada, ln1_g, ln1_b, ln2_g, ln2_b,
              w_router_group, b_router_group, w_router_expert, b_router_expert,
              w_gate, w_up, w_down):
    seq_p = x_prompt.shape[1]
    seq_s = x_sample.shape[1]
    pos_p = jnp.arange(seq_p)
    pos_s = PAST_LEN + jnp.arange(seq_s)
    len_pa = min(A_WINDOW_MAX, seq_p)
    len_pb = min(WINDOW_B - 1, seq_p)
    len_sa = min(A_WINDOW_MAX, cache_a_k.shape[2] + seq_s)
    len_sb = min(WINDOW_B - 1, cache_b_k.shape[2] + seq_s)
    xp, xs = x_prompt, x_sample
    pa_k, pa_v, pb_k, pb_v, sa_k, sa_v, sb_k, sb_v = [], [], [], [], [], [], [], []
    for l in range(DEPTH):
        sink = attn_sinks[l].reshape(KV_B, G_B)
        moe_w = (w_router_group[l], b_router_group[l], w_router_expert[l], b_router_expert[l],
                 w_gate[l], w_up[l], w_down[l])
        sh1, sc1, g1, sh2, sc2, g2 = adaln_modulation(c_prompt, w_ada[l], b_ada[l])
        q_a, k_a, v_a, q_b, k_b, v_b = project_in(xp * (1 + sc1) + sh1, w_in[l], pos_p)
        o_a = dilated_attention_prompt(q_a, k_a, v_a)
        o_b, _ = banded_attention(q_b, k_b, v_b, WINDOW_B - 1, sink)
        xp = layer_norm(DN_ALPHA * xp + g1 * project_out(o_a, o_b, w_out[l]), ln1_g[l], ln1_b[l])
        xp = layer_norm(DN_ALPHA * xp + g2 * hierarchical_moe(xp * (1 + sc2) + sh2, *moe_w),
                        ln2_g[l], ln2_b[l])
        pa_k.append(k_a[:, seq_p - len_pa:])
        pa_v.append(v_a[:, seq_p - len_pa:])
        pb_k.append(k_b[:, seq_p - len_pb:])
        pb_v.append(v_b[:, seq_p - len_pb:])
        sh1, sc1, g1, sh2, sc2, g2 = adaln_modulation(c_sample, w_ada[l], b_ada[l])
        q_a, k_a, v_a, q_b, k_b, v_b = project_in(xs * (1 + sc1) + sh1, w_in[l], pos_s)
        ka_all = jnp.concatenate([cache_a_k[l].astype(k_a.dtype), k_a], axis=1)
        va_all = jnp.concatenate([cache_a_v[l].astype(v_a.dtype), v_a], axis=1)
        kb_all = jnp.concatenate([cache_b_k[l].astype(k_b.dtype), k_b], axis=1)
        vb_all = jnp.concatenate([cache_b_v[l].astype(v_b.dtype), v_b], axis=1)
        o_a = dilated_attention_sample(q_a, ka_all, va_all)
        o_b = swa_sink_attention_sample(q_b, kb_all, vb_all, sink)
        xs = layer_norm(DN_ALPHA * xs + g1 * project_out(o_a, o_b, w_out[l]), ln1_g[l], ln1_b[l])
        xs = layer_norm(DN_ALPHA * xs + g2 * hierarchical_moe(xs * (1 + sc2) + sh2, *moe_w),
                        ln2_g[l], ln2_b[l])
        sa_k.append(ka_all[:, ka_all.shape[1] - len_sa:])
        sa_v.append(va_all[:, va_all.shape[1] - len_sa:])
        sb_k.append(kb_all[:, kb_all.shape[1] - len_sb:])
        sb_v.append(vb_all[:, vb_all.shape[1] - len_sb:])
    return (xp, xs,
            jnp.stack(pa_k), jnp.stack(pa_v), jnp.stack(pb_k), jnp.stack(pb_v),
            jnp.stack(sa_k), jnp.stack(sa_v), jnp.stack(sb_k), jnp.stack(sb_v))
```

```python
import functools

import jax
import jax.numpy as jnp
import numpy as np
from jax import lax
from jax.experimental import pallas as pl
from jax.experimental.pallas import tpu as pltpu

f32 = jnp.float32
bf16 = jnp.bfloat16

D = 1024
B = 4
T = 4096
L = 4
NS = 128
S = 8
PAST = 8192
HD = 64
H_A = 8
H_B = 8
KV_B = 2
PATTERNS = ((128, 1), (512, 4), (2048, 16))
W_A = 2048
W_B = 127
WIN_B = 128
N_GROUPS = 4
E_PER_G = 4
N_EXP = 16
D_EXP = 512
THETA = 10000.0
EPS = 1e-5
NEG = -1e30
SCALE = HD ** -0.5
ALPHA = (2 * L) ** 0.25
D_QA = 512
D_IN = 2304
NP_TOK = B * T
NS_TOK = NS * S
N_TOK = NP_TOK + NS_TOK
BLK = 128

TM = 512
NT_P = NP_TOK // TM
NT = N_TOK // TM
TM_MOE = 512
NT_MOE = N_TOK // TM_MOE
LANES = 128
ROUTER_LANES = 128
VMEM_LIMIT = 56 * 1024 * 1024


def _cparams(sem):
    return pltpu.CompilerParams(dimension_semantics=sem, vmem_limit_bytes=VMEM_LIMIT)


def _mod_body(c_ref, w_ref, b_ref, o_ref):
    c = c_ref[...]
    a = (c * jax.nn.sigmoid(c)).astype(bf16)
    o_ref[...] = jnp.dot(a, w_ref[...], preferred_element_type=f32) + b_ref[...]


def _modulation(c_all, w_ada, b_ada):
    R = c_all.shape[0]
    return pl.pallas_call(
        _mod_body,
        grid=(L, 6),
        in_specs=[pl.BlockSpec((R, D), lambda l, j: (0, 0)),
                  pl.BlockSpec((None, D, D), lambda l, j: (l, 0, j)),
                  pl.BlockSpec((None, 1, D), lambda l, j: (l, 0, j))],
        out_specs=pl.BlockSpec((None, R, D), lambda l, j: (l, 0, j)),
        out_shape=jax.ShapeDtypeStruct((L, R, 6 * D), f32),
        compiler_params=_cparams(("arbitrary", "arbitrary")),
        name="modulation",
    )(c_all, w_ada, b_ada)


def _mod_specs(l, chunk, tm):
    nt_p, tiles_per_batch = NP_TOK // tm, T // tm
    p = pl.BlockSpec((None, None, 1, D), lambda i, *_: (l, jnp.minimum(i // tiles_per_batch, B - 1), 0, chunk))
    s = pl.BlockSpec((None, tm, D), lambda i, *_: (l, jnp.maximum(i - nt_p, 0), chunk))
    return p, s


def _pick_mod(i, nt_p, mp_ref, ms_ref):
    return jnp.where(i >= nt_p, ms_ref[...], mp_ref[...])


def _swap_halves(x):
    lane = lax.broadcasted_iota(jnp.int32, x.shape, 1)
    return jnp.where((lane % HD) < (HD // 2), pltpu.roll(x, LANES - HD // 2, axis=1), pltpu.roll(x, HD // 2, axis=1))


def _modproj_body(x_ref, shp_ref, shs_ref, scp_ref, scs_ref, cos_ref, sin_ref, w_ref,
                  qa_ref, ka_ref, va_ref, qb_ref, kbd_ref, vbd_ref, kb_ref, vb_ref):
    i = pl.program_id(0)
    sh = _pick_mod(i, NT_P, shp_ref, shs_ref)
    sc = _pick_mod(i, NT_P, scp_ref, scs_ref)
    h = (x_ref[...] * (1.0 + sc) + sh).astype(bf16)
    r = jnp.dot(h, w_ref[...], preferred_element_type=f32)
    cos = cos_ref[...]
    sin = sin_ref[...]

    def rope(c0, width):
        parts = []
        for c in range(c0, c0 + width, LANES):
            x = r[:, c:c + LANES]
            parts.append(x * cos + _swap_halves(x) * sin)
        return parts

    qa = rope(0, D_QA)
    ka = rope(D_QA, D_QA)
    qb = rope(3 * D_QA, D_QA)
    kb = rope(4 * D_QA, LANES)[0]
    vb = r[:, 4 * D_QA + LANES:4 * D_QA + 2 * LANES]
    qa_ref[...] = jnp.concatenate(qa, axis=1)
    ka_ref[...] = jnp.concatenate(ka, axis=1)
    va_ref[...] = r[:, 2 * D_QA:3 * D_QA]
    qb_ref[...] = jnp.concatenate(qb, axis=1).astype(bf16)
    kb_ref[...] = kb
    vb_ref[...] = vb

    def dup(x):
        lane = lax.broadcasted_iota(jnp.int32, x.shape, 1)
        rolled = pltpu.roll(x, HD, axis=1)
        lo = jnp.where(lane < HD, x, rolled)
        hi = jnp.where(lane < HD, rolled, x)
        return jnp.concatenate([lo, hi], axis=1).astype(bf16)

    kbd_ref[...] = dup(kb)
    vbd_ref[...] = dup(vb)


def _modproj(l, x_all, modp, mods, cos_t, sin_t, w_in):
    shp, shs = _mod_specs(l, 0, TM)
    scp, scs = _mod_specs(l, 1, TM)
    nrope = T // TM

    def rope_idx(i):
        return (jnp.where(i < NT_P, i % nrope, nrope + i - NT_P), 0)

    tok = lambda w: pl.BlockSpec((TM, w), lambda i: (i, 0))
    sds = lambda w, dt: jax.ShapeDtypeStruct((N_TOK, w), dt)
    return pl.pallas_call(
        _modproj_body,
        grid=(NT,),
        in_specs=[tok(D), shp, shs, scp, scs,
                  pl.BlockSpec((TM, LANES), rope_idx), pl.BlockSpec((TM, LANES), rope_idx),
                  pl.BlockSpec((None, D, D_IN), lambda i: (l, 0, 0))],
        out_specs=[tok(D_QA), tok(D_QA), tok(D_QA), tok(D_QA), tok(2 * LANES), tok(2 * LANES), tok(LANES), tok(LANES)],
        out_shape=[sds(D_QA, f32), sds(D_QA, f32), sds(D_QA, f32), sds(D_QA, bf16),
                   sds(2 * LANES, bf16), sds(2 * LANES, bf16), sds(LANES, f32), sds(LANES, f32)],
        compiler_params=_cparams(("arbitrary",)),
        name="modproj",
    )(x_all, modp, mods, modp, mods, cos_t, sin_t, w_in)


def _pair_attention(q, k, v, off, sink_col=None):
    lane = lax.broadcasted_iota(jnp.int32, (BLK, LANES), 1)
    zero = jnp.zeros_like(q)
    qs = jnp.concatenate([jnp.where(lane < HD, q, zero), jnp.where(lane >= HD, q, zero)], axis=0).astype(bf16)
    s = lax.dot_general(qs, k, (((1,), (1,)), ((), ())), preferred_element_type=f32) * SCALE
    row = lax.broadcasted_iota(jnp.int32, (2 * BLK, 2 * BLK), 0) % BLK
    col = lax.broadcasted_iota(jnp.int32, (2 * BLK, 2 * BLK), 1)
    dist = off + row - col
    s = jnp.where((dist >= 0) & (dist <= BLK), s, NEG) if sink_col is None else \
        jnp.where((dist >= 0) & (dist < WIN_B), s, NEG)
    m = jnp.max(s, axis=1, keepdims=True)
    if sink_col is not None:
        m = jnp.maximum(m, sink_col)
    p = jnp.exp(s - m)
    den = jnp.sum(p, axis=1, keepdims=True)
    if sink_col is not None:
        den = den + jnp.exp(sink_col - m)
    o = jnp.dot(p.astype(bf16), v, preferred_element_type=f32) / den
    lse = jnp.broadcast_to(m + jnp.log(den), (2 * BLK, LANES))
    o_pair = jnp.where(lane < HD, o[:BLK], o[BLK:])
    lse_pair = jnp.where(lane < HD, lse[:BLK], lse[BLK:])
    return o_pair, lse_pair


def _attn_a_p_body(q_ref, k_ref, v_ref, o_ref, acc_ref, m_ref, s_ref):
    for pi, (window, dil) in enumerate(PATTERNS):
        seq = T // dil
        nblk = seq // BLK

        def block(idx, _, dil=dil, nblk=nblk, first=(pi == 0)):
            c = idx // nblk
            j = idx % nblk
            kj = jnp.maximum(j - 1, 0)
            off = (j - kj) * BLK
            if dil == 1:
                qrows = pl.ds(pl.multiple_of(j * BLK, BLK), BLK)
                krows = pl.ds(pl.multiple_of(kj * BLK, BLK), 2 * BLK)
            else:
                qrows = pl.ds(c + dil * BLK * j, BLK, stride=dil)
                krows = pl.ds(c + dil * BLK * kj, 2 * BLK, stride=dil)
            q = q_ref[qrows, :]
            k = k_ref[krows, :].astype(bf16)
            v = v_ref[krows, :].astype(bf16)
            o, lse = _pair_attention(q, k, v, off)
            if first:
                acc_ref[qrows, :] = o
                m_ref[qrows, :] = lse
                s_ref[qrows, :] = jnp.ones_like(lse)
            else:
                m_old = m_ref[qrows, :]
                m_new = jnp.maximum(m_old, lse)
                a = jnp.exp(m_old - m_new)
                b = jnp.exp(lse - m_new)
                acc_ref[qrows, :] = acc_ref[qrows, :] * a + o * b
                s_ref[qrows, :] = s_ref[qrows, :] * a + b
                m_ref[qrows, :] = m_new
            return 0

        lax.fori_loop(0, T // BLK, block, 0)
    o_ref[...] = (acc_ref[...] / s_ref[...]).astype(o_ref.dtype)


def _attn_a_prompt(qa, ka, va):
    spec = pl.BlockSpec((T, LANES), lambda b, p: (b, p))
    return pl.pallas_call(
        _attn_a_p_body,
        grid=(B, D_QA // LANES),
        in_specs=[spec, spec, spec],
        out_specs=spec,
        out_shape=jax.ShapeDtypeStruct((NP_TOK, D_QA), bf16),
        scratch_shapes=[pltpu.VMEM((T, LANES), f32)] * 3,
        compiler_params=_cparams(("arbitrary", "arbitrary")),
        name="attn_a_prompt",
    )(qa, ka, va)


def _attn_b_p_body(l, sink_ref, q_ref, k_ref, v_ref, o_ref):
    p = pl.program_id(1)
    rowi = lax.broadcasted_iota(jnp.int32, (2 * BLK, 1), 0)
    sink_col = jnp.where(rowi < BLK, sink_ref[l, 2 * p], sink_ref[l, 2 * p + 1])

    def block(j, _):
        kj = jnp.maximum(j - 1, 0)
        off = (j - kj) * BLK
        qrows = pl.ds(pl.multiple_of(j * BLK, BLK), BLK)
        krows = pl.ds(pl.multiple_of(kj * BLK, BLK), 2 * BLK)
        o, _ = _pair_attention(q_ref[qrows, :], k_ref[krows, :], v_ref[krows, :], off, sink_col)
        o_ref[qrows, :] = o.astype(o_ref.dtype)
        return 0

    lax.fori_loop(0, T // BLK, block, 0)


def _attn_b_prompt(l, sinks, qb, kbd, vbd):
    qspec = pl.BlockSpec((T, LANES), lambda b, p: (b, p))
    kspec = pl.BlockSpec((T, LANES), lambda b, p: (b, p // 2))
    return pl.pallas_call(
        functools.partial(_attn_b_p_body, l),
        grid=(B, D_QA // LANES),
        in_specs=[pl.BlockSpec(memory_space=pltpu.SMEM), qspec, kspec, kspec],
        out_specs=qspec,
        out_shape=jax.ShapeDtypeStruct((NP_TOK, D_QA), bf16),
        compiler_params=_cparams(("arbitrary", "arbitrary")),
        name="attn_b_prompt",
    )(sinks, qb, kbd, vbd)


_A_RANGES = tuple(W_A - max(window, 2 * LANES) for window, _ in PATTERNS)


def _attn_a_s_body(q_ref, kn_ref, vn_ref, kc_ref, vc_ref, kp_ref, vp_ref, o_ref, ok_ref, ov_ref):
    q = q_ref[...]
    kn = kn_ref[...]
    vn = vn_ref[...]
    srow = lax.broadcasted_iota(jnp.int32, (S, W_A), 0)
    jcol = lax.broadcasted_iota(jnp.int32, (S, W_A), 1)
    delta_c = W_A + srow - jcol
    sr = lax.broadcasted_iota(jnp.int32, (S, S), 0)
    jc = lax.broadcasted_iota(jnp.int32, (S, S), 1)
    delta_n = sr - jc
    masks_c = [((delta_c & (dil - 1)) == 0) & (delta_c <= window) for window, dil in PATTERNS]
    masks_n = [((delta_n & (dil - 1)) == 0) & (delta_n >= 0) & (delta_n <= window) for window, dil in PATTERNS]
    lane_last = lax.broadcasted_iota(jnp.int32, (HD, LANES), 1)
    outs = []
    for h in range(H_A):
        qh = q[:, h * HD:(h + 1) * HD].astype(bf16)
        knh = kn[:, h * HD:(h + 1) * HD].astype(bf16)
        vnh = vn[:, h * HD:(h + 1) * HD].astype(bf16)
        kch = kc_ref[h]
        vch = vc_ref[h]
        kcb = kch.astype(bf16)
        vcb = vch.astype(bf16)
        s_c = jnp.dot(qh, kcb, preferred_element_type=f32) * SCALE
        s_n = lax.dot_general(qh, knh, (((1,), (1,)), ((), ())), preferred_element_type=f32) * SCALE
        acc = m_run = s_run = None
        for pi in range(len(PATTERNS)):
            c0 = _A_RANGES[pi]
            sc = jnp.where(masks_c[pi][:, c0:], s_c[:, c0:], NEG)
            sn = jnp.where(masks_n[pi], s_n, NEG)
            m = jnp.maximum(jnp.max(sc, axis=1, keepdims=True), jnp.max(sn, axis=1, keepdims=True))
            pc = jnp.exp(sc - m)
            pn = jnp.exp(sn - m)
            den = jnp.sum(pc, axis=1, keepdims=True) + jnp.sum(pn, axis=1, keepdims=True)
            o = (lax.dot_general(pc.astype(bf16), vcb[:, c0:], (((1,), (1,)), ((), ())), preferred_element_type=f32)
                 + jnp.dot(pn.astype(bf16), vnh, preferred_element_type=f32)) / den
            lse = m + jnp.log(den)
            if acc is None:
                acc, m_run, s_run = o, lse, jnp.ones_like(lse)
            else:
                m_new = jnp.maximum(m_run, lse)
                a = jnp.exp(m_run - m_new)
                b = jnp.exp(lse - m_new)
                acc = acc * a + o * b
                s_run = s_run * a + b
                m_run = m_new
        outs.append(acc / s_run)
        for c_ref, p_ref, dst in ((kch, kp_ref, ok_ref), (vch, vp_ref, ov_ref)):
            rolled = pltpu.roll(c_ref, W_A - S, axis=1)
            dst[h, :, :W_A - LANES] = rolled[:, :W_A - LANES]
            dst[h, :, W_A - LANES:] = jnp.where(lane_last >= LANES - S, p_ref[h], rolled[:, W_A - LANES:])
    o_ref[...] = jnp.concatenate(outs, axis=1).astype(o_ref.dtype)


def _attn_a_sample(l, qa, ka, va, cak_t, cav_t, knew_t, vnew_t, prev):
    row0 = NP_TOK // S
    tok = pl.BlockSpec((S, D_QA), lambda n: (row0 + n, 0))
    cspec = pl.BlockSpec((None, None, H_A, HD, W_A), lambda n: (l, n, 0, 0, 0))
    pspec = pl.BlockSpec((None, H_A, HD, LANES), lambda n: (n, 0, 0, 0))
    big = jax.ShapeDtypeStruct((L, NS, H_A, HD, W_A), f32)
    args = [qa, ka, va, cak_t, cav_t, knew_t, vnew_t]
    in_specs = [tok, tok, tok, cspec, cspec, pspec, pspec]
    aliases = {}
    body = _attn_a_s_body
    if prev is not None:
        args += list(prev)
        in_specs += [pl.BlockSpec(memory_space=pl.ANY)] * 2
        aliases = {7: 1, 8: 2}
        body = lambda *refs: _attn_a_s_body(*refs[:7], *refs[9:])
    return pl.pallas_call(
        body,
        grid=(NS,),
        in_specs=in_specs,
        out_specs=[pl.BlockSpec((S, D_QA), lambda n: (n, 0)), cspec, cspec],
        out_shape=[jax.ShapeDtypeStruct((NS_TOK, D_QA), f32), big, big],
        input_output_aliases=aliases,
        compiler_params=_cparams(("arbitrary",)),
        name="attn_a_sample",
    )(*args)


NB_S = 16
KB_ALL = 136


def _attn_b_s_body(l, sink_ref, qb_ref, k_ref, v_ref, o_ref, q_ref):
    q_ref[...] = qb_ref[...].astype(f32)
    lane = lax.broadcasted_iota(jnp.int32, (KB_ALL, LANES), 1)
    lane_q = lax.broadcasted_iota(jnp.int32, (S, LANES), 1)
    row = lax.broadcasted_iota(jnp.int32, (2 * S, KB_ALL), 0) % S
    col = lax.broadcasted_iota(jnp.int32, (2 * S, KB_ALL), 1)
    dist = W_B + row - col
    valid = (dist >= 0) & (dist < WIN_B) & (col < W_B + S)
    rowi = lax.broadcasted_iota(jnp.int32, (2 * S, 1), 0)

    def seq(n, _):
        k = k_ref[n]
        v = v_ref[n]
        kr = pltpu.roll(k, HD, axis=1)
        vr = pltpu.roll(v, HD, axis=1)
        rows = pl.ds(pl.multiple_of(n * S, S), S)
        outs = []
        for p in range(D_QA // LANES):
            g = p // 2
            kd = (jnp.where(lane < HD, k, kr) if g == 0 else jnp.where(lane < HD, kr, k)).astype(bf16)
            vd = (jnp.where(lane < HD, v, vr) if g == 0 else jnp.where(lane < HD, vr, v)).astype(bf16)
            q = q_ref[rows, p * LANES:(p + 1) * LANES]
            zero = jnp.zeros_like(q)
            qs = jnp.concatenate([jnp.where(lane_q < HD, q, zero), jnp.where(lane_q >= HD, q, zero)], axis=0).astype(bf16)
            s = lax.dot_general(qs, kd, (((1,), (1,)), ((), ())), preferred_element_type=f32) * SCALE
            s = jnp.where(valid, s, NEG)
            sink_col = jnp.where(rowi < S, sink_ref[l, 2 * p], sink_ref[l, 2 * p + 1])
            m = jnp.maximum(jnp.max(s, axis=1, keepdims=True), sink_col)
            pr = jnp.exp(s - m)
            den = jnp.sum(pr, axis=1, keepdims=True) + jnp.exp(sink_col - m)
            o = jnp.dot(pr.astype(bf16), vd, preferred_element_type=f32) / den
            outs.append(jnp.where(lane_q < HD, o[:S], o[S:]))
        o_ref[rows, :] = jnp.concatenate(outs, axis=1).astype(o_ref.dtype)
        return 0

    lax.fori_loop(0, NB_S, seq, 0)


def _attn_b_sample(l, sinks, qb, kall, vall):
    blk0 = NP_TOK // (NB_S * S)
    cspec = pl.BlockSpec((NB_S, KB_ALL, LANES), lambda i: (i, 0, 0))
    return pl.pallas_call(
        functools.partial(_attn_b_s_body, l),
        grid=(NS // NB_S,),
        in_specs=[pl.BlockSpec(memory_space=pltpu.SMEM),
                  pl.BlockSpec((NB_S * S, D_QA), lambda i: (blk0 + i, 0)), cspec, cspec],
        out_specs=pl.BlockSpec((NB_S * S, D_QA), lambda i: (i, 0)),
        out_shape=jax.ShapeDtypeStruct((NS_TOK, D_QA), f32),
        scratch_shapes=[pltpu.VMEM((NB_S * S, D_QA), f32)],
        compiler_params=_cparams(("arbitrary",)),
        name="attn_b_sample",
    )(sinks, qb, kall, vall)


def _layer_norm(x, g, b):
    mu = jnp.mean(x, axis=-1, keepdims=True)
    xc = x - mu
    var = jnp.mean(xc * xc, axis=-1, keepdims=True)
    return xc * lax.rsqrt(var + EPS) * g + b


def _outproj_body(x_ref, oap_ref, oas_ref, obp_ref, obs_ref, g1p_ref, g1s_ref, sh2p_ref, sh2s_ref, sc2p_ref, sc2s_ref,
                  w_ref, lng_ref, lnb_ref, wr_ref, br_ref, x1_ref, h2_ref, gate_ref):
    i = pl.program_id(0)
    smp = i >= NT_P
    oa = jnp.where(smp, oas_ref[...].astype(bf16), oap_ref[...])
    ob = jnp.where(smp, obs_ref[...].astype(bf16), obp_ref[...])
    y = jnp.dot(jnp.concatenate([oa, ob], axis=1), w_ref[...], preferred_element_type=f32)
    g1 = _pick_mod(i, NT_P, g1p_ref, g1s_ref)
    x1 = _layer_norm(ALPHA * x_ref[...] + g1 * y, lng_ref[...], lnb_ref[...])
    x1_ref[...] = x1
    h2 = x1 * (1.0 + _pick_mod(i, NT_P, sc2p_ref, sc2s_ref)) + _pick_mod(i, NT_P, sh2p_ref, sh2s_ref)
    h2_ref[...] = h2.astype(bf16)
    logit = jnp.dot(h2, wr_ref[...], precision=lax.Precision.HIGHEST, preferred_element_type=f32) + br_ref[...]
    lane = lax.broadcasted_iota(jnp.int32, logit.shape, 1)
    big = jnp.int32(1 << 20)
    is_g = lane < N_GROUPS
    lg = jnp.where(is_g, logit, NEG)
    gmax = jnp.max(lg, axis=1, keepdims=True)
    g_sel = jnp.min(jnp.where(is_g & (lg == gmax), lane, big), axis=1, keepdims=True)
    p_group = 1.0 / jnp.sum(jnp.where(is_g, jnp.exp(lg - gmax), 0.0), axis=1, keepdims=True)
    e_idx = lane - N_GROUPS
    in_g = (lane >= N_GROUPS) & (lane < N_GROUPS + N_EXP) & ((e_idx // E_PER_G) == g_sel)
    le = jnp.where(in_g, logit, -jnp.inf)
    v1 = jnp.max(le, axis=1, keepdims=True)
    i1 = jnp.min(jnp.where(in_g & (le == v1), lane, big), axis=1, keepdims=True)
    le2 = jnp.where(lane == i1, -jnp.inf, le)
    v2 = jnp.max(le2, axis=1, keepdims=True)
    i2 = jnp.min(jnp.where(in_g & (le2 == v2), lane, big), axis=1, keepdims=True)
    e2 = jnp.exp(v2 - v1)
    w1 = (1.0 / (1.0 + e2)) * p_group
    w2 = (e2 / (1.0 + e2)) * p_group
    gate_ref[...] = jnp.where(lane == i1, w1, 0.0) + jnp.where(lane == i2, w2, 0.0)


def _outproj(l, x_all, oa_p, oa_s, ob_p, ob_s, modp, mods, w_out, ln_g, ln_b, w_r, b_r):
    tok = lambda w: pl.BlockSpec((TM, w), lambda i: (i, 0))
    tok_p = pl.BlockSpec((TM, D_QA), lambda i: (jnp.minimum(i, NT_P - 1), 0))
    tok_s = pl.BlockSpec((TM, D_QA), lambda i: (jnp.maximum(i - NT_P, 0), 0))
    g1 = _mod_specs(l, 2, TM)
    sh2 = _mod_specs(l, 3, TM)
    sc2 = _mod_specs(l, 4, TM)
    full = lambda r, c: pl.BlockSpec((None, r, c), lambda i: (l, 0, 0))
    return pl.pallas_call(
        _outproj_body,
        grid=(NT,),
        in_specs=[tok(D), tok_p, tok_s, tok_p, tok_s, *g1, *sh2, *sc2,
                  full(D, D), full(1, D), full(1, D), full(D, ROUTER_LANES), full(1, ROUTER_LANES)],
        out_specs=[tok(D), tok(D), tok(ROUTER_LANES)],
        out_shape=[jax.ShapeDtypeStruct((N_TOK, D), f32), jax.ShapeDtypeStruct((N_TOK, D), bf16),
                   jax.ShapeDtypeStruct((N_TOK, ROUTER_LANES), f32)],
        compiler_params=_cparams(("arbitrary",)),
        name="outproj",
    )(x_all, oa_p, oa_s, ob_p, ob_s, modp, mods, modp, mods, modp, mods, w_out, ln_g, ln_b, w_r, b_r)


def _moe_body(h_ref, gate_ref, wg_ref, wu_ref, wd_ref, x1_ref, g2p_ref, g2s_ref, lng_ref, lnb_ref, o_ref, acc_ref):
    i = pl.program_id(0)
    e = pl.program_id(1)
    h = h_ref[...]
    gt = jnp.dot(h, wg_ref[...], preferred_element_type=f32)
    up = jnp.dot(h, wu_ref[...], preferred_element_type=f32)
    a = (gt * jax.nn.sigmoid(gt)) * up
    y = jnp.dot(a.astype(bf16), wd_ref[...], preferred_element_type=f32)
    lane = lax.broadcasted_iota(jnp.int32, (TM_MOE, ROUTER_LANES), 1)
    ge = jnp.sum(jnp.where(lane == e + N_GROUPS, gate_ref[...], 0.0), axis=1, keepdims=True)
    contrib = ge * y

    @pl.when(e == 0)
    def _():
        acc_ref[...] = contrib

    @pl.when(e > 0)
    def _():
        acc_ref[...] += contrib

    @pl.when(e == N_EXP - 1)
    def _():
        g2 = _pick_mod(i, NP_TOK // TM_MOE, g2p_ref, g2s_ref)
        o_ref[...] = _layer_norm(ALPHA * x1_ref[...] + g2 * acc_ref[...], lng_ref[...], lnb_ref[...])


def _moe(l, h2, gates, wg, wu, wd, x1, modp, mods, ln_g, ln_b):
    tok = lambda w: pl.BlockSpec((TM_MOE, w), lambda i, e: (i, 0))
    g2p, g2s = _mod_specs(l, 5, TM_MOE)
    full = pl.BlockSpec((None, 1, D), lambda i, e: (l, 0, 0))
    return pl.pallas_call(
        _moe_body,
        grid=(NT_MOE, N_EXP),
        in_specs=[tok(D), tok(ROUTER_LANES),
                  pl.BlockSpec((None, None, D, D_EXP), lambda i, e: (l, e, 0, 0)),
                  pl.BlockSpec((None, None, D, D_EXP), lambda i, e: (l, e, 0, 0)),
                  pl.BlockSpec((None, None, D_EXP, D), lambda i, e: (l, e, 0, 0)),
                  tok(D), g2p, g2s, full, full],
        out_specs=tok(D),
        out_shape=jax.ShapeDtypeStruct((N_TOK, D), f32),
        scratch_shapes=[pltpu.VMEM((TM_MOE, D), f32)],
        compiler_params=_cparams(("arbitrary", "arbitrary")),
        name="moe",
    )(h2, gates, wg, wu, wd, x1, modp, mods, ln_g, ln_b)


def _rope_tables():
    half = HD // 2
    inv_freq = THETA ** (-jnp.arange(half, dtype=f32) / half)
    pos = jnp.concatenate([jnp.arange(T), jnp.tile(PAST + jnp.arange(S), NS)]).astype(f32)
    ang = pos[:, None] * inv_freq[None, :]
    cos, sin = jnp.cos(ang), jnp.sin(ang)
    cos_t = jnp.tile(jnp.concatenate([cos, cos], axis=1), (1, LANES // HD))
    sin_t = jnp.tile(jnp.concatenate([-sin, sin], axis=1), (1, LANES // HD))
    return cos_t, sin_t


def kernel(x_prompt, x_sample, cache_a_k, cache_a_v, cache_b_k, cache_b_v, c_prompt, c_sample, w_in, w_out, attn_sinks, w_ada, b_ada, ln1_g, ln1_b, ln2_g, ln2_b, w_router_group, b_router_group, w_router_expert, b_router_expert, w_gate, w_up, w_down):
    assert x_prompt.shape == (B, T, D) and x_sample.shape == (NS, S, D)
    assert cache_a_k.shape == (L, NS, W_A, H_A, HD) and cache_b_k.shape == (L, NS, W_B, KV_B, HD)
    x_all = jnp.concatenate([x_prompt.reshape(NP_TOK, D), x_sample.reshape(NS_TOK, D)], axis=0)
    c_all = jnp.concatenate([jnp.repeat(c_sample, S, axis=0), c_prompt, jnp.zeros((4, D), f32)], axis=0)
    mod = _modulation(c_all, w_ada.astype(bf16), b_ada.reshape(L, 1, 6 * D))
    cos_t, sin_t = _rope_tables()
    w_in_b, w_out_b = w_in.astype(bf16), w_out.astype(bf16)
    wg_b, wu_b, wd_b = w_gate.astype(bf16), w_up.astype(bf16), w_down.astype(bf16)
    pad = ROUTER_LANES - N_GROUPS - N_EXP
    w_r = jnp.concatenate([w_router_group, w_router_expert, jnp.zeros((L, D, pad), f32)], axis=2)
    b_r = jnp.concatenate([b_router_group, b_router_expert, jnp.zeros((L, pad), f32)], axis=1).reshape(L, 1, ROUTER_LANES)
    modp = mod[:, NS_TOK:NS_TOK + B].reshape(L, B, 1, 6 * D)
    ln1g, ln1b, ln2g, ln2b = (a.reshape(L, 1, D) for a in (ln1_g, ln1_b, ln2_g, ln2_b))
    cak_t = jnp.transpose(cache_a_k, (0, 1, 3, 4, 2))
    cav_t = jnp.transpose(cache_a_v, (0, 1, 3, 4, 2))

    pa_k, pa_v, pb_k, pb_v, sb_k, sb_v = [], [], [], [], [], []
    sa = None
    for l in range(L):
        qa, ka, va, qb, kbd, vbd, kb, vb = _modproj(l, x_all, modp, mod, cos_t, sin_t, w_in_b)
        oa_p = _attn_a_prompt(qa, ka, va)
        ob_p = _attn_b_prompt(l, attn_sinks, qb, kbd, vbd)

        def placed(a):
            a = a[NP_TOK:].reshape(NS, S, H_A, HD).transpose(0, 2, 3, 1)
            return jnp.pad(a, ((0, 0), (0, 0), (0, 0), (LANES - S, 0)))

        oa_s, sak, sav = _attn_a_sample(l, qa, ka, va, cak_t, cav_t, placed(ka), placed(va), sa)
        sa = (sak, sav)

        def with_new(cache_l, new):
            new = new[NP_TOK:].reshape(NS, S, LANES)
            return jnp.concatenate([cache_l.reshape(NS, W_B, LANES), new, jnp.zeros((NS, 1, LANES), f32)], axis=1)

        kall = with_new(cache_b_k[l], kb)
        vall = with_new(cache_b_v[l], vb)
        ob_s = _attn_b_sample(l, attn_sinks, qb, kall, vall)

        x1, h2, gates = _outproj(l, x_all, oa_p, oa_s, ob_p, ob_s, modp, mod, w_out_b, ln1g, ln1b, w_r, b_r)
        x_all = _moe(l, h2, gates, wg_b, wu_b, wd_b, x1, modp, mod, ln2g, ln2b)

        pa_k.append(ka[:NP_TOK].reshape(B, T, H_A, HD)[:, T - W_A:])
        pa_v.append(va[:NP_TOK].reshape(B, T, H_A, HD)[:, T - W_A:])
        pb_k.append(kb[:NP_TOK].reshape(B, T, KV_B, HD)[:, T - W_B:])
        pb_v.append(vb[:NP_TOK].reshape(B, T, KV_B, HD)[:, T - W_B:])
        sb_k.append(kall[:, S:S + W_B].reshape(NS, W_B, KV_B, HD))
        sb_v.append(vall[:, S:S + W_B].reshape(NS, W_B, KV_B, HD))

    y_prompt = x_all[:NP_TOK].reshape(B, T, D)
    y_sample = x_all[NP_TOK:].reshape(NS, S, D)
    sa_k = jnp.transpose(sa[0], (0, 1, 4, 2, 3))
    sa_v = jnp.transpose(sa[1], (0, 1, 4, 2, 3))
    return (y_prompt, y_sample, jnp.stack(pa_k), jnp.stack(pa_v), jnp.stack(pb_k), jnp.stack(pb_v),
            sa_k, sa_v, jnp.stack(sb_k), jnp.stack(sb_v))
```

```python
import functools

import jax
import jax.numpy as jnp
import numpy as np
from jax import lax
from jax.experimental import pallas as pl
from jax.experimental.pallas import tpu as pltpu

f32 = jnp.float32
bf16 = jnp.bfloat16

D = 1024
B = 4
T = 4096
L = 4
NS = 128
S = 8
PAST = 8192
HD = 64
H_A = 8
H_B = 8
KV_B = 2
PATTERNS = ((128, 1), (512, 4), (2048, 16))
W_A = 2048
W_B = 127
WIN_B = 128
N_GROUPS = 4
E_PER_G = 4
N_EXP = 16
D_EXP = 512
THETA = 10000.0
EPS = 1e-5
NEG = -1e30
SCALE = HD ** -0.5
ALPHA = (2 * L) ** 0.25
D_QA = 512
D_IN = 2304
NP_TOK = B * T
NS_TOK = NS * S
N_TOK = NP_TOK + NS_TOK
BLK = 128
UNROLL = 4

TM = 512
NT_P = NP_TOK // TM
NT = N_TOK // TM
D_EXT = D + 128
PAIRS = ((0, 1), (2, 1), (2, 3), (0, 3), (0, 2), (1, 3))
N_CLS = N_GROUPS * len(PAIRS)
TMS = 256
NT_S = -(-N_TOK // TMS) + N_CLS
LANES = 128
ROUTER_LANES = 128
VMEM_LIMIT = 56 * 1024 * 1024


def _cparams(sem):
    return pltpu.CompilerParams(dimension_semantics=sem, vmem_limit_bytes=VMEM_LIMIT)


def _mod_body(c_ref, w_ref, b_ref, o_ref):
    c = c_ref[...]
    a = (c * jax.nn.sigmoid(c)).astype(bf16)
    o_ref[...] = jnp.dot(a, w_ref[...], preferred_element_type=f32) + b_ref[...]


def _modulation(c_all, w_ada, b_ada):
    R = c_all.shape[0]
    return pl.pallas_call(
        _mod_body,
        grid=(L, 6),
        in_specs=[pl.BlockSpec((R, D), lambda l, j: (0, 0)),
                  pl.BlockSpec((None, D, D), lambda l, j: (l, 0, j)),
                  pl.BlockSpec((None, 1, D), lambda l, j: (l, 0, j))],
        out_specs=pl.BlockSpec((None, R, D), lambda l, j: (l, 0, j)),
        out_shape=jax.ShapeDtypeStruct((L, R, 6 * D), f32),
        compiler_params=_cparams(("arbitrary", "arbitrary")),
        name="modulation",
    )(c_all, w_ada, b_ada)


def _mod_specs(l, chunk, tm):
    nt_p, tiles_per_batch = NP_TOK // tm, T // tm
    p = pl.BlockSpec((None, None, 1, D), lambda i, *_: (l, jnp.minimum(i // tiles_per_batch, B - 1), 0, chunk))
    s = pl.BlockSpec((None, tm, D), lambda i, *_: (l, jnp.maximum(i - nt_p, 0), chunk))
    return p, s


def _pick_mod(i, nt_p, mp_ref, ms_ref):
    return jnp.where(i >= nt_p, ms_ref[...], mp_ref[...])


def _swap_halves(x):
    lane = lax.broadcasted_iota(jnp.int32, x.shape, 1)
    return jnp.where((lane % HD) < (HD // 2), pltpu.roll(x, LANES - HD // 2, axis=1), pltpu.roll(x, HD // 2, axis=1))


def _modproj_body(x_ref, shp_ref, shs_ref, scp_ref, scs_ref, cos_ref, sin_ref, w_ref,
                  qa_ref, ka_ref, va_ref, qb_ref, kbd_ref, vbd_ref, kb_ref, vb_ref):
    i = pl.program_id(0)
    sh = _pick_mod(i, NT_P, shp_ref, shs_ref)
    sc = _pick_mod(i, NT_P, scp_ref, scs_ref)
    h = (x_ref[...] * (1.0 + sc) + sh).astype(bf16)
    r = jnp.dot(h, w_ref[...], preferred_element_type=f32)
    cos = cos_ref[...]
    sin = sin_ref[...]

    def rope(c0, width):
        parts = []
        for c in range(c0, c0 + width, LANES):
            x = r[:, c:c + LANES]
            parts.append(x * cos + _swap_halves(x) * sin)
        return parts

    qa = rope(0, D_QA)
    ka = rope(D_QA, D_QA)
    qb = rope(3 * D_QA, D_QA)
    kb = rope(4 * D_QA, LANES)[0]
    vb = r[:, 4 * D_QA + LANES:4 * D_QA + 2 * LANES]
    qa_ref[...] = jnp.concatenate(qa, axis=1)
    ka_ref[...] = jnp.concatenate(ka, axis=1)
    va_ref[...] = r[:, 2 * D_QA:3 * D_QA]
    qb_ref[...] = jnp.concatenate(qb, axis=1).astype(bf16)
    kb_ref[...] = kb
    vb_ref[...] = vb

    def dup(x):
        lane = lax.broadcasted_iota(jnp.int32, x.shape, 1)
        rolled = pltpu.roll(x, HD, axis=1)
        lo = jnp.where(lane < HD, x, rolled)
        hi = jnp.where(lane < HD, rolled, x)
        return jnp.concatenate([lo, hi], axis=1).astype(bf16)

    kbd_ref[...] = dup(kb)
    vbd_ref[...] = dup(vb)


def _modproj(l, x_all, modp, mods, cos_t, sin_t, w_in):
    shp, shs = _mod_specs(l, 0, TM)
    scp, scs = _mod_specs(l, 1, TM)
    nrope = T // TM

    def rope_idx(i):
        return (jnp.where(i < NT_P, i % nrope, nrope + i - NT_P), 0)

    tok = lambda w: pl.BlockSpec((TM, w), lambda i: (i, 0))
    sds = lambda w, dt: jax.ShapeDtypeStruct((N_TOK, w), dt)
    return pl.pallas_call(
        _modproj_body,
        grid=(NT,),
        in_specs=[tok(D), shp, shs, scp, scs,
                  pl.BlockSpec((TM, LANES), rope_idx), pl.BlockSpec((TM, LANES), rope_idx),
                  pl.BlockSpec((None, D, D_IN), lambda i: (l, 0, 0))],
        out_specs=[tok(D_QA), tok(D_QA), tok(D_QA), tok(D_QA), tok(2 * LANES), tok(2 * LANES), tok(LANES), tok(LANES)],
        out_shape=[sds(D_QA, f32), sds(D_QA, f32), sds(D_QA, f32), sds(D_QA, bf16),
                   sds(2 * LANES, bf16), sds(2 * LANES, bf16), sds(LANES, f32), sds(LANES, f32)],
        compiler_params=_cparams(("arbitrary",)),
        name="modproj",
    )(x_all, modp, mods, modp, mods, cos_t, sin_t, w_in)


def _pair_attention(q, k, v, off, sink_col=None):
    lane = lax.broadcasted_iota(jnp.int32, (BLK, LANES), 1)
    zero = jnp.zeros_like(q)
    qs = jnp.concatenate([jnp.where(lane < HD, q, zero), jnp.where(lane >= HD, q, zero)], axis=0).astype(bf16)
    s = lax.dot_general(qs, k, (((1,), (1,)), ((), ())), preferred_element_type=f32) * SCALE
    row = lax.broadcasted_iota(jnp.int32, (2 * BLK, 2 * BLK), 0) % BLK
    col = lax.broadcasted_iota(jnp.int32, (2 * BLK, 2 * BLK), 1)
    dist = off + row - col
    s = jnp.where((dist >= 0) & (dist <= BLK), s, NEG) if sink_col is None else \
        jnp.where((dist >= 0) & (dist < WIN_B), s, NEG)
    m = jnp.max(s, axis=1, keepdims=True)
    if sink_col is not None:
        m = jnp.maximum(m, sink_col)
    p = jnp.exp(s - m)
    den = jnp.sum(p, axis=1, keepdims=True)
    if sink_col is not None:
        den = den + jnp.exp(sink_col - m)
    o = jnp.dot(p.astype(bf16), v, preferred_element_type=f32) / den
    lse = jnp.broadcast_to(m + jnp.log(den), (2 * BLK, LANES))
    o_pair = jnp.where(lane < HD, o[:BLK], o[BLK:])
    lse_pair = jnp.where(lane < HD, lse[:BLK], lse[BLK:])
    return o_pair, lse_pair


def _attn_a_p_body(q_ref, k_ref, v_ref, o_ref, acc_ref, m_ref, s_ref):
    for pi, (window, dil) in enumerate(PATTERNS):
        nblk = T // dil // BLK
        first = pi == 0

        def blocks(it, _, dil=dil, nblk=nblk, first=first):
            work = []
            for u in range(UNROLL):
                idx = it * UNROLL + u
                c = idx // nblk
                j = idx % nblk
                kj = jnp.maximum(j - 1, 0)
                off = (j - kj) * BLK
                if dil == 1:
                    qrows = pl.ds(pl.multiple_of(j * BLK, BLK), BLK)
                    krows = pl.ds(pl.multiple_of(kj * BLK, BLK), 2 * BLK)
                else:
                    qrows = pl.ds(c + dil * BLK * j, BLK, stride=dil)
                    krows = pl.ds(c + dil * BLK * kj, 2 * BLK, stride=dil)
                o, lse = _pair_attention(q_ref[qrows, :], k_ref[krows, :].astype(bf16), v_ref[krows, :].astype(bf16), off)
                if first:
                    work.append((qrows, o, lse, jnp.ones_like(lse)))
                else:
                    m_old = m_ref[qrows, :]
                    m_new = jnp.maximum(m_old, lse)
                    a = jnp.exp(m_old - m_new)
                    b = jnp.exp(lse - m_new)
                    work.append((qrows, acc_ref[qrows, :] * a + o * b, m_new, s_ref[qrows, :] * a + b))
            for qrows, acc, m_new, s_new in work:
                acc_ref[qrows, :] = acc
                m_ref[qrows, :] = m_new
                s_ref[qrows, :] = s_new
            return 0

        lax.fori_loop(0, T // BLK // UNROLL, blocks, 0)
    o_ref[...] = (acc_ref[...] / s_ref[...]).astype(o_ref.dtype)


def _attn_a_prompt(qa, ka, va):
    spec = pl.BlockSpec((T, LANES), lambda b, p: (b, p))
    return pl.pallas_call(
        _attn_a_p_body,
        grid=(B, D_QA // LANES),
        in_specs=[spec, spec, spec],
        out_specs=spec,
        out_shape=jax.ShapeDtypeStruct((NP_TOK, D_QA), bf16),
        scratch_shapes=[pltpu.VMEM((T, LANES), f32)] * 3,
        compiler_params=_cparams(("arbitrary", "arbitrary")),
        name="attn_a_prompt",
    )(qa, ka, va)


def _attn_b_p_body(l, sink_ref, q_ref, k_ref, v_ref, o_ref):
    p = pl.program_id(1)
    rowi = lax.broadcasted_iota(jnp.int32, (2 * BLK, 1), 0)
    sink_col = jnp.where(rowi < BLK, sink_ref[l, 2 * p], sink_ref[l, 2 * p + 1])

    def blocks(it, _):
        work = []
        for u in range(UNROLL):
            j = it * UNROLL + u
            kj = jnp.maximum(j - 1, 0)
            off = (j - kj) * BLK
            qrows = pl.ds(pl.multiple_of(j * BLK, BLK), BLK)
            krows = pl.ds(pl.multiple_of(kj * BLK, BLK), 2 * BLK)
            o, _ = _pair_attention(q_ref[qrows, :], k_ref[krows, :], v_ref[krows, :], off, sink_col)
            work.append((qrows, o))
        for qrows, o in work:
            o_ref[qrows, :] = o.astype(o_ref.dtype)
        return 0

    lax.fori_loop(0, T // BLK // UNROLL, blocks, 0)


def _attn_b_prompt(l, sinks, qb, kbd, vbd):
    qspec = pl.BlockSpec((T, LANES), lambda b, p: (b, p))
    kspec = pl.BlockSpec((T, LANES), lambda b, p: (b, p // 2))
    return pl.pallas_call(
        functools.partial(_attn_b_p_body, l),
        grid=(B, D_QA // LANES),
        in_specs=[pl.BlockSpec(memory_space=pltpu.SMEM), qspec, kspec, kspec],
        out_specs=qspec,
        out_shape=jax.ShapeDtypeStruct((NP_TOK, D_QA), bf16),
        compiler_params=_cparams(("arbitrary", "arbitrary")),
        name="attn_b_prompt",
    )(sinks, qb, kbd, vbd)


TC = 256


def _cache_t_body(k_ref, v_ref, ok_ref, ov_ref):
    ok_ref[...] = k_ref[...].T
    ov_ref[...] = v_ref[...].T


def _cache_t(ka, va):
    nb = W_A // TC
    ispec = pl.BlockSpec((TC, D_QA), lambda b, j: (b * (T // TC) + (T - W_A) // TC + j, 0))
    ospec = pl.BlockSpec((None, D_QA, TC), lambda b, j: (b, 0, j))
    sds = jax.ShapeDtypeStruct((B, D_QA, W_A), f32)
    return pl.pallas_call(
        _cache_t_body,
        grid=(B, nb),
        in_specs=[ispec, ispec],
        out_specs=[ospec, ospec],
        out_shape=[sds, sds],
        compiler_params=_cparams(("arbitrary", "arbitrary")),
        name="cache_t",
    )(ka, va)


_A_RANGES = tuple(W_A - max(window, 2 * LANES) for window, _ in PATTERNS)


def _place_new_rows(new_h):
    col = lax.broadcasted_iota(jnp.int32, (S, LANES), 1)
    row = lax.broadcasted_iota(jnp.int32, (S, LANES), 0)
    sel = jnp.where(col == LANES - S + row, 1.0, 0.0).astype(bf16)
    hi = new_h.astype(bf16)
    r1 = new_h - hi.astype(f32)
    mid = r1.astype(bf16)
    lo = (r1 - mid.astype(f32)).astype(bf16)
    dn = (((0,), (0,)), ((), ()))
    return (lax.dot_general(hi, sel, dn, preferred_element_type=f32)
            + lax.dot_general(mid, sel, dn, preferred_element_type=f32)
            + lax.dot_general(lo, sel, dn, preferred_element_type=f32))


def _attn_a_s_body(q_ref, kn_ref, vn_ref, kc_ref, vc_ref, o_ref, ok_ref, ov_ref):
    q = q_ref[...]
    kn = kn_ref[...]
    vn = vn_ref[...]
    srow = lax.broadcasted_iota(jnp.int32, (S, W_A), 0)
    jcol = lax.broadcasted_iota(jnp.int32, (S, W_A), 1)
    delta_c = W_A + srow - jcol
    sr = lax.broadcasted_iota(jnp.int32, (S, S), 0)
    jc = lax.broadcasted_iota(jnp.int32, (S, S), 1)
    delta_n = sr - jc
    masks_c = [((delta_c & (dil - 1)) == 0) & (delta_c <= window) for window, dil in PATTERNS]
    masks_n = [((delta_n & (dil - 1)) == 0) & (delta_n >= 0) & (delta_n <= window) for window, dil in PATTERNS]
    lane_last = lax.broadcasted_iota(jnp.int32, (HD, LANES), 1)
    outs = []
    for h in range(H_A):
        qh = q[:, h * HD:(h + 1) * HD].astype(bf16)
        knf = kn[:, h * HD:(h + 1) * HD]
        vnf = vn[:, h * HD:(h + 1) * HD]
        knh = knf.astype(bf16)
        vnh = vnf.astype(bf16)
        kch = kc_ref[h]
        vch = vc_ref[h]
        kcb = kch.astype(bf16)
        vcb = vch.astype(bf16)
        s_c = jnp.dot(qh, kcb, preferred_element_type=f32) * SCALE
        s_n = lax.dot_general(qh, knh, (((1,), (1,)), ((), ())), preferred_element_type=f32) * SCALE
        acc = m_run = s_run = None
        for pi in range(len(PATTERNS)):
            c0 = _A_RANGES[pi]
            sc = jnp.where(masks_c[pi][:, c0:], s_c[:, c0:], NEG)
            sn = jnp.where(masks_n[pi], s_n, NEG)
            m = jnp.maximum(jnp.max(sc, axis=1, keepdims=True), jnp.max(sn, axis=1, keepdims=True))
            pc = jnp.exp(sc - m)
            pn = jnp.exp(sn - m)
            den = jnp.sum(pc, axis=1, keepdims=True) + jnp.sum(pn, axis=1, keepdims=True)
            o = (lax.dot_general(pc.astype(bf16), vcb[:, c0:], (((1,), (1,)), ((), ())), preferred_element_type=f32)
                 + jnp.dot(pn.astype(bf16), vnh, preferred_element_type=f32)) / den
            lse = m + jnp.log(den)
            if acc is None:
                acc, m_run, s_run = o, lse, jnp.ones_like(lse)
            else:
                m_new = jnp.maximum(m_run, lse)
                a = jnp.exp(m_run - m_new)
                b = jnp.exp(lse - m_new)
                acc = acc * a + o * b
                s_run = s_run * a + b
                m_run = m_new
        outs.append(acc / s_run)
        for cache_h, new_h, dst in ((kch, knf, ok_ref), (vch, vnf, ov_ref)):
            rolled = pltpu.roll(cache_h, W_A - S, axis=1)
            dst[h, :, :W_A - LANES] = rolled[:, :W_A - LANES]
            dst[h, :, W_A - LANES:] = jnp.where(lane_last >= LANES - S, _place_new_rows(new_h), rolled[:, W_A - LANES:])
    o_ref[...] = jnp.concatenate(outs, axis=1).astype(o_ref.dtype)


def _attn_a_sample(l, qa, ka, va, cak_t, cav_t, prev):
    row0 = NP_TOK // S
    tok = pl.BlockSpec((S, D_QA), lambda n: (row0 + n, 0))
    cspec = pl.BlockSpec((None, None, H_A, HD, W_A), lambda n: (l, n, 0, 0, 0))
    big = jax.ShapeDtypeStruct((L, NS, H_A, HD, W_A), f32)
    args = [qa, ka, va, cak_t, cav_t]
    in_specs = [tok, tok, tok, cspec, cspec]
    aliases = {}
    body = _attn_a_s_body
    if prev is not None:
        args += list(prev)
        in_specs += [pl.BlockSpec(memory_space=pl.ANY)] * 2
        aliases = {5: 1, 6: 2}
        body = lambda *refs: _attn_a_s_body(*refs[:5], *refs[7:])
    return pl.pallas_call(
        body,
        grid=(NS,),
        in_specs=in_specs,
        out_specs=[pl.BlockSpec((S, D_QA), lambda n: (n, 0)), cspec, cspec],
        out_shape=[jax.ShapeDtypeStruct((NS_TOK, D_QA), f32), big, big],
        input_output_aliases=aliases,
        compiler_params=_cparams(("arbitrary",)),
        name="attn_a_sample",
    )(*args)


NB_S = 16
KB_ALL = 136


def _attn_b_s_body(l, sink_ref, qb_ref, k_ref, v_ref, o_ref, q_ref):
    q_ref[...] = qb_ref[...].astype(f32)
    lane = lax.broadcasted_iota(jnp.int32, (KB_ALL, LANES), 1)
    lane_q = lax.broadcasted_iota(jnp.int32, (S, LANES), 1)
    row = lax.broadcasted_iota(jnp.int32, (2 * S, KB_ALL), 0) % S
    col = lax.broadcasted_iota(jnp.int32, (2 * S, KB_ALL), 1)
    dist = W_B + row - col
    valid = (dist >= 0) & (dist < WIN_B) & (col < W_B + S)
    rowi = lax.broadcasted_iota(jnp.int32, (2 * S, 1), 0)

    def seq(n, _):
        k = k_ref[n]
        v = v_ref[n]
        kr = pltpu.roll(k, HD, axis=1)
        vr = pltpu.roll(v, HD, axis=1)
        rows = pl.ds(pl.multiple_of(n * S, S), S)
        outs = []
        for p in range(D_QA // LANES):
            g = p // 2
            kd = (jnp.where(lane < HD, k, kr) if g == 0 else jnp.where(lane < HD, kr, k)).astype(bf16)
            vd = (jnp.where(lane < HD, v, vr) if g == 0 else jnp.where(lane < HD, vr, v)).astype(bf16)
            q = q_ref[rows, p * LANES:(p + 1) * LANES]
            zero = jnp.zeros_like(q)
            qs = jnp.concatenate([jnp.where(lane_q < HD, q, zero), jnp.where(lane_q >= HD, q, zero)], axis=0).astype(bf16)
            s = lax.dot_general(qs, kd, (((1,), (1,)), ((), ())), preferred_element_type=f32) * SCALE
            s = jnp.where(valid, s, NEG)
            sink_col = jnp.where(rowi < S, sink_ref[l, 2 * p], sink_ref[l, 2 * p + 1])
            m = jnp.maximum(jnp.max(s, axis=1, keepdims=True), sink_col)
            pr = jnp.exp(s - m)
            den = jnp.sum(pr, axis=1, keepdims=True) + jnp.exp(sink_col - m)
            o = jnp.dot(pr.astype(bf16), vd, preferred_element_type=f32) / den
            outs.append(jnp.where(lane_q < HD, o[:S], o[S:]))
        o_ref[rows, :] = jnp.concatenate(outs, axis=1).astype(o_ref.dtype)
        return 0

    lax.fori_loop(0, NB_S, seq, 0)


def _attn_b_sample(l, sinks, qb, kall, vall):
    blk0 = NP_TOK // (NB_S * S)
    cspec = pl.BlockSpec((NB_S, KB_ALL, LANES), lambda i: (i, 0, 0))
    return pl.pallas_call(
        functools.partial(_attn_b_s_body, l),
        grid=(NS // NB_S,),
        in_specs=[pl.BlockSpec(memory_space=pltpu.SMEM),
                  pl.BlockSpec((NB_S * S, D_QA), lambda i: (blk0 + i, 0)), cspec, cspec],
        out_specs=pl.BlockSpec((NB_S * S, D_QA), lambda i: (i, 0)),
        out_shape=jax.ShapeDtypeStruct((NS_TOK, D_QA), f32),
        scratch_shapes=[pltpu.VMEM((NB_S * S, D_QA), f32)],
        compiler_params=_cparams(("arbitrary",)),
        name="attn_b_sample",
    )(sinks, qb, kall, vall)


def _layer_norm(x, g, b):
    mu = jnp.mean(x, axis=-1, keepdims=True)
    xc = x - mu
    var = jnp.mean(xc * xc, axis=-1, keepdims=True)
    return xc * lax.rsqrt(var + EPS) * g + b


def _split_bf16(x):
    hi = x.astype(bf16)
    return hi, (x - hi.astype(f32)).astype(bf16)


def _outproj_body(x_ref, oap_ref, oas_ref, obp_ref, obs_ref, g1p_ref, g1s_ref, sh2p_ref, sh2s_ref, sc2p_ref, sc2s_ref,
                  w_ref, lng_ref, lnb_ref, wr_ref, br_ref, x1_ref, hx_ref):
    i = pl.program_id(0)
    smp = i >= NT_P
    oa = jnp.where(smp, oas_ref[...].astype(bf16), oap_ref[...])
    ob = jnp.where(smp, obs_ref[...].astype(bf16), obp_ref[...])
    y = jnp.dot(jnp.concatenate([oa, ob], axis=1), w_ref[...], preferred_element_type=f32)
    g1 = _pick_mod(i, NT_P, g1p_ref, g1s_ref)
    x1 = _layer_norm(ALPHA * x_ref[...] + g1 * y, lng_ref[...], lnb_ref[...])
    x1_ref[...] = x1
    h2 = x1 * (1.0 + _pick_mod(i, NT_P, sc2p_ref, sc2s_ref)) + _pick_mod(i, NT_P, sh2p_ref, sh2s_ref)
    h_hi, h_lo = _split_bf16(h2)
    w_hi, w_lo = _split_bf16(wr_ref[...])
    logit = (jnp.dot(h_hi, w_hi, preferred_element_type=f32) + jnp.dot(h_hi, w_lo, preferred_element_type=f32)
             + jnp.dot(h_lo, w_hi, preferred_element_type=f32)) + br_ref[...]
    lt = logit.T
    lg = lt[0:N_GROUPS]
    le = lt[N_GROUPS:N_GROUPS + N_EXP]
    gi = lax.broadcasted_iota(jnp.int32, lg.shape, 0)
    ei = lax.broadcasted_iota(jnp.int32, le.shape, 0)
    big = jnp.int32(1 << 20)
    gmax = jnp.max(lg, axis=0, keepdims=True)
    g_sel = jnp.min(jnp.where(lg == gmax, gi, big), axis=0, keepdims=True)
    p_group = 1.0 / jnp.sum(jnp.exp(lg - gmax), axis=0, keepdims=True)
    in_g = (ei // E_PER_G) == g_sel
    lem = jnp.where(in_g, le, -jnp.inf)
    v1 = jnp.max(lem, axis=0, keepdims=True)
    i1 = jnp.min(jnp.where(in_g & (lem == v1), ei, big), axis=0, keepdims=True)
    lem2 = jnp.where(ei == i1, -jnp.inf, lem)
    v2 = jnp.max(lem2, axis=0, keepdims=True)
    i2 = jnp.min(jnp.where(in_g & (lem2 == v2), ei, big), axis=0, keepdims=True)
    e2 = jnp.exp(v2 - v1)
    w1 = (1.0 / (1.0 + e2)) * p_group
    w2 = (e2 / (1.0 + e2)) * p_group
    a1 = i1 - g_sel * E_PER_G
    a2 = i2 - g_sel * E_PER_G
    code = jnp.minimum(a1, a2) * E_PER_G + jnp.maximum(a1, a2)
    pair = jnp.zeros_like(code)
    slot_a = jnp.zeros_like(code)
    for k, (pa, pb) in enumerate(PAIRS):
        hit = code == min(pa, pb) * E_PER_G + max(pa, pb)
        pair = jnp.where(hit, k, pair)
        slot_a = jnp.where(hit, pa, slot_a)
    w_a = jnp.where(a1 == slot_a, w1, w2)
    w_b = jnp.where(a1 == slot_a, w2, w1)
    cls = (g_sel * len(PAIRS) + pair).astype(f32)
    ri = lax.broadcasted_iota(jnp.int32, (LANES, TM), 0)
    tail_t = jnp.where(ri == 0, w_a, 0.0) + jnp.where(ri == 1, w_b, 0.0) + jnp.where(ri == 2, cls, 0.0)
    hx_ref[...] = jnp.concatenate([h2, tail_t.T], axis=1)


def _outproj(l, x_all, oa_p, oa_s, ob_p, ob_s, modp, mods, w_out, ln_g, ln_b, w_r, b_r):
    tok = lambda w: pl.BlockSpec((TM, w), lambda i: (i, 0))
    tok_p = pl.BlockSpec((TM, D_QA), lambda i: (jnp.minimum(i, NT_P - 1), 0))
    tok_s = pl.BlockSpec((TM, D_QA), lambda i: (jnp.maximum(i - NT_P, 0), 0))
    g1 = _mod_specs(l, 2, TM)
    sh2 = _mod_specs(l, 3, TM)
    sc2 = _mod_specs(l, 4, TM)
    full = lambda r, c: pl.BlockSpec((None, r, c), lambda i: (l, 0, 0))
    return pl.pallas_call(
        _outproj_body,
        grid=(NT,),
        in_specs=[tok(D), tok_p, tok_s, tok_p, tok_s, *g1, *sh2, *sc2,
                  full(D, D), full(1, D), full(1, D), full(D, ROUTER_LANES), full(1, ROUTER_LANES)],
        out_specs=[tok(D), tok(D_EXT)],
        out_shape=[jax.ShapeDtypeStruct((N_TOK, D), f32), jax.ShapeDtypeStruct((N_TOK, D_EXT), f32)],
        compiler_params=_cparams(("arbitrary",)),
        name="outproj",
    )(x_all, oa_p, oa_s, ob_p, ob_s, modp, mods, modp, mods, modp, mods, w_out, ln_g, ln_b, w_r, b_r)


def _route(hx):
    cls = hx[:, D + 2].astype(jnp.int32)
    order = jnp.argsort(cls, stable=True).astype(jnp.int32)
    counts = jnp.sum((cls[:, None] == jnp.arange(N_CLS)[None, :]).astype(jnp.int32), axis=0)
    ntile = (counts + TMS - 1) // TMS
    tile_end = jnp.cumsum(ntile)
    n_used = tile_end[-1]
    t = jnp.arange(NT_S)
    tile_cls = jnp.searchsorted(tile_end, jnp.minimum(t, n_used - 1), side="right").astype(jnp.int32)
    row_off = (tile_end - ntile) * TMS
    tok_off = jnp.cumsum(counts) - counts
    j = jnp.arange(NT_S * TMS)
    cj = tile_cls[j // TMS]
    rj = j - row_off[cj]
    valid = (rj >= 0) & (rj < counts[cj]) & (j // TMS < n_used)
    tok = order[jnp.clip(tok_off[cj] + rj, 0, N_TOK - 1)]
    src_g = jnp.where(valid, tok, 0).astype(jnp.int32).reshape(NT_S, 1, TMS)
    src_s = jnp.where(valid, tok, N_TOK + j).astype(jnp.int32).reshape(NT_S, 1, TMS)
    grp = tile_cls // len(PAIRS)
    pair = tile_cls % len(PAIRS)
    e_a = (grp * E_PER_G + jnp.asarray([p[0] for p in PAIRS], jnp.int32)[pair]).astype(jnp.int32)
    e_b = (grp * E_PER_G + jnp.asarray([p[1] for p in PAIRS], jnp.int32)[pair]).astype(jnp.int32)
    return src_g, src_s, e_a, e_b, n_used.reshape(1).astype(jnp.int32)


def _row_copy_in(hx_hbm, idx_ref, buf, sem, r):
    return pltpu.make_async_copy(hx_hbm.at[pl.ds(idx_ref[0, r], 1), :], buf.at[pl.ds(r, 1), :], sem)


def _row_copy_out(buf, y_hbm, idx_ref, sem, r):
    return pltpu.make_async_copy(buf.at[pl.ds(r, 1), :], y_hbm.at[pl.ds(idx_ref[0, r], 1), :], sem)


def _experts_body(ea_ref, eb_ref, nu_ref, g_cur_ref, g_nxt_ref, s_cur_ref, s_prv_ref, s_pp_ref, hx_hbm,
                  wga_ref, wua_ref, wda_ref, wgb_ref, wub_ref, wdb_ref, y_hbm, xbuf, obuf, gsem, ssem):
    t = pl.program_id(0)
    n_used = nu_ref[0]
    slot = t % 2

    def gather(idx_ref, sl, go):
        def one(r, _):
            cp = _row_copy_in(hx_hbm, idx_ref, xbuf.at[sl], gsem.at[sl], r)
            cp.start() if go else cp.wait()
            return 0
        lax.fori_loop(0, TMS, one, 0, unroll=8)

    def scatter(idx_ref, sl, go):
        def one(r, _):
            cp = _row_copy_out(obuf.at[sl], y_hbm, idx_ref, ssem.at[sl], r)
            cp.start() if go else cp.wait()
            return 0
        lax.fori_loop(0, TMS, one, 0, unroll=8)

    @pl.when((t == 0) & (n_used > 0))
    def _():
        gather(g_cur_ref, 0, True)

    @pl.when(t + 1 < n_used)
    def _():
        gather(g_nxt_ref, 1 - slot, True)

    @pl.when((t >= 2) & (t - 2 < n_used))
    def _():
        scatter(s_pp_ref, slot, False)

    @pl.when(t < n_used)
    def _():
        gather(g_cur_ref, slot, False)
        x = xbuf[slot]
        xb = x[:, :D].astype(bf16)
        out = None
        for wcol, wg_ref, wu_ref, wd_ref in ((0, wga_ref, wua_ref, wda_ref), (1, wgb_ref, wub_ref, wdb_ref)):
            gt = jnp.dot(xb, wg_ref[...], preferred_element_type=f32)
            up = jnp.dot(xb, wu_ref[...], preferred_element_type=f32)
            a = ((gt * jax.nn.sigmoid(gt)) * up).astype(bf16)
            yo = x[:, D + wcol:D + wcol + 1] * jnp.dot(a, wd_ref[...], preferred_element_type=f32)
            out = yo if out is None else out + yo
        obuf[slot] = out
        scatter(s_cur_ref, slot, True)

    @pl.when(t == NT_S - 1)
    def _():
        @pl.when((t >= 1) & (t - 1 < n_used))
        def _():
            scatter(s_prv_ref, 1 - slot, False)

        @pl.when(t < n_used)
        def _():
            scatter(s_cur_ref, slot, False)


def _experts(l, hx, route, wg, wu, wd):
    src_g, src_s, e_a, e_b, n_used = route
    smem = lambda f: pl.BlockSpec((None, 1, TMS), f, memory_space=pltpu.SMEM)
    last = NT_S - 1
    wspec = lambda e_idx, r, c: pl.BlockSpec((None, None, r, c), lambda t, ea, eb, nu: (l, (ea, eb)[e_idx][t], 0, 0))
    grid_spec = pltpu.PrefetchScalarGridSpec(
        num_scalar_prefetch=3,
        grid=(NT_S,),
        in_specs=[smem(lambda t, *_: (t, 0, 0)), smem(lambda t, *_: (jnp.minimum(t + 1, last), 0, 0)),
                  smem(lambda t, *_: (t, 0, 0)), smem(lambda t, *_: (jnp.maximum(t - 1, 0), 0, 0)),
                  smem(lambda t, *_: (jnp.maximum(t - 2, 0), 0, 0)),
                  pl.BlockSpec(memory_space=pl.ANY),
                  wspec(0, D, D_EXP), wspec(0, D, D_EXP), wspec(0, D_EXP, D),
                  wspec(1, D, D_EXP), wspec(1, D, D_EXP), wspec(1, D_EXP, D)],
        out_specs=pl.BlockSpec(memory_space=pl.ANY),
        scratch_shapes=[pltpu.VMEM((2, TMS, D_EXT), f32), pltpu.VMEM((2, TMS, D), f32),
                        pltpu.SemaphoreType.DMA((2,)), pltpu.SemaphoreType.DMA((2,))],
    )
    return pl.pallas_call(
        _experts_body,
        grid_spec=grid_spec,
        out_shape=jax.ShapeDtypeStruct((N_TOK + NT_S * TMS, D), f32),
        compiler_params=_cparams(("arbitrary",)),
        name="experts",
    )(e_a, e_b, n_used, src_g, src_g, src_s, src_s, src_s, hx, wg, wu, wd, wg, wu, wd)


def _ln2_body(x1_ref, y_ref, g2p_ref, g2s_ref, lng_ref, lnb_ref, o_ref):
    i = pl.program_id(0)
    g2 = _pick_mod(i, NT_P, g2p_ref, g2s_ref)
    o_ref[...] = _layer_norm(ALPHA * x1_ref[...] + g2 * y_ref[...], lng_ref[...], lnb_ref[...])


def _ln2(l, x1, y, modp, mods, ln_g, ln_b):
    tok = pl.BlockSpec((TM, D), lambda i: (i, 0))
    g2 = _mod_specs(l, 5, TM)
    full = pl.BlockSpec((None, 1, D), lambda i: (l, 0, 0))
    return pl.pallas_call(
        _ln2_body,
        grid=(NT,),
        in_specs=[tok, tok, *g2, full, full],
        out_specs=tok,
        out_shape=jax.ShapeDtypeStruct((N_TOK, D), f32),
        compiler_params=_cparams(("arbitrary",)),
        name="ln2",
    )(x1, y, modp, mods, ln_g, ln_b)


def _rope_tables():
    half = HD // 2
    inv_freq = THETA ** (-jnp.arange(half, dtype=f32) / half)
    pos = jnp.concatenate([jnp.arange(T), jnp.tile(PAST + jnp.arange(S), NS)]).astype(f32)
    ang = pos[:, None] * inv_freq[None, :]
    cos, sin = jnp.cos(ang), jnp.sin(ang)
    cos_t = jnp.tile(jnp.concatenate([cos, cos], axis=1), (1, LANES // HD))
    sin_t = jnp.tile(jnp.concatenate([-sin, sin], axis=1), (1, LANES // HD))
    return cos_t, sin_t


def kernel(x_prompt, x_sample, cache_a_k, cache_a_v, cache_b_k, cache_b_v, c_prompt, c_sample, w_in, w_out, attn_sinks, w_ada, b_ada, ln1_g, ln1_b, ln2_g, ln2_b, w_router_group, b_router_group, w_router_expert, b_router_expert, w_gate, w_up, w_down):
    assert x_prompt.shape == (B, T, D) and x_sample.shape == (NS, S, D)
    assert cache_a_k.shape == (L, NS, W_A, H_A, HD) and cache_b_k.shape == (L, NS, W_B, KV_B, HD)
    x_all = jnp.concatenate([x_prompt.reshape(NP_TOK, D), x_sample.reshape(NS_TOK, D)], axis=0)
    c_all = jnp.concatenate([jnp.repeat(c_sample, S, axis=0), c_prompt, jnp.zeros((4, D), f32)], axis=0)
    mod = _modulation(c_all, w_ada.astype(bf16), b_ada.reshape(L, 1, 6 * D))
    cos_t, sin_t = _rope_tables()
    w_in_b, w_out_b = w_in.astype(bf16), w_out.astype(bf16)
    wg_b, wu_b, wd_b = w_gate.astype(bf16), w_up.astype(bf16), w_down.astype(bf16)
    pad = ROUTER_LANES - N_GROUPS - N_EXP
    w_r = jnp.concatenate([w_router_group, w_router_expert, jnp.zeros((L, D, pad), f32)], axis=2)
    b_r = jnp.concatenate([b_router_group, b_router_expert, jnp.zeros((L, pad), f32)], axis=1).reshape(L, 1, ROUTER_LANES)
    modp = mod[:, NS_TOK:NS_TOK + B].reshape(L, B, 1, 6 * D)
    ln1g, ln1b, ln2g, ln2b = (a.reshape(L, 1, D) for a in (ln1_g, ln1_b, ln2_g, ln2_b))
    cak_t = jnp.transpose(cache_a_k, (0, 1, 3, 4, 2))
    cav_t = jnp.transpose(cache_a_v, (0, 1, 3, 4, 2))

    pa_k, pa_v, pb_k, pb_v, sb_k, sb_v = [], [], [], [], [], []
    sa = None
    for l in range(L):
        qa, ka, va, qb, kbd, vbd, kb, vb = _modproj(l, x_all, modp, mod, cos_t, sin_t, w_in_b)
        oa_p = _attn_a_prompt(qa, ka, va)
        ob_p = _attn_b_prompt(l, attn_sinks, qb, kbd, vbd)
        oa_s, sak, sav = _attn_a_sample(l, qa, ka, va, cak_t, cav_t, sa)
        sa = (sak, sav)

        def with_new(cache_l, new):
            new = new[NP_TOK:].reshape(NS, S, LANES)
            return jnp.concatenate([cache_l.reshape(NS, W_B, LANES), new, jnp.zeros((NS, 1, LANES), f32)], axis=1)

        kall = with_new(cache_b_k[l], kb)
        vall = with_new(cache_b_v[l], vb)
        ob_s = _attn_b_sample(l, attn_sinks, qb, kall, vall)

        x1, hx = _outproj(l, x_all, oa_p, oa_s, ob_p, ob_s, modp, mod, w_out_b, ln1g, ln1b, w_r, b_r)
        y = _experts(l, hx, _route(hx), wg_b, wu_b, wd_b)
        x_all = _ln2(l, x1, y, modp, mod, ln2g, ln2b)

        pak_t, pav_t = _cache_t(ka, va)
        pa_k.append(pak_t.reshape(B, H_A, HD, W_A))
        pa_v.append(pav_t.reshape(B, H_A, HD, W_A))
        pb_k.append(kb[:NP_TOK].reshape(B, T, KV_B, HD)[:, T - W_B:])
        pb_v.append(vb[:NP_TOK].reshape(B, T, KV_B, HD)[:, T - W_B:])
        sb_k.append(kall[:, S:S + W_B].reshape(NS, W_B, KV_B, HD))
        sb_v.append(vall[:, S:S + W_B].reshape(NS, W_B, KV_B, HD))

    y_prompt = x_all[:NP_TOK].reshape(B, T, D)
    y_sample = x_all[NP_TOK:].reshape(NS, S, D)
    sa_k = jnp.transpose(sa[0], (0, 1, 4, 2, 3))
    sa_v = jnp.transpose(sa[1], (0, 1, 4, 2, 3))
    pa_k = jnp.transpose(jnp.stack(pa_k), (0, 1, 4, 2, 3))
    pa_v = jnp.transpose(jnp.stack(pa_v), (0, 1, 4, 2, 3))
    return (y_prompt, y_sample, pa_k, pa_v, jnp.stack(pb_k), jnp.stack(pb_v),
            sa_k, sa_v, jnp.stack(sb_k), jnp.stack(sb_v))
```

```python
import functools

import jax
import jax.numpy as jnp
import numpy as np
from jax import lax
from jax.experimental import pallas as pl
from jax.experimental.pallas import tpu as pltpu

f32 = jnp.float32
bf16 = jnp.bfloat16

D = 1024
B = 4
T = 4096
L = 4
NS = 128
S = 8
PAST = 8192
HD = 64
H_A = 8
H_B = 8
KV_B = 2
PATTERNS = ((128, 1), (512, 4), (2048, 16))
W_A = 2048
W_B = 127
WIN_B = 128
N_GROUPS = 4
E_PER_G = 4
N_EXP = 16
D_EXP = 512
THETA = 10000.0
EPS = 1e-5
NEG = -1e30
SCALE = HD ** -0.5
ALPHA = (2 * L) ** 0.25
D_QA = 512
D_IN = 2304
NP_TOK = B * T
NS_TOK = NS * S
N_TOK = NP_TOK + NS_TOK
BLK = 128
UNROLL = 4

TM = 512
NT_P = NP_TOK // TM
NT = N_TOK // TM
D_EXT = D + 128
PAIRS = ((0, 1), (2, 1), (2, 3), (0, 3), (0, 2), (1, 3))
N_CLS = N_GROUPS * len(PAIRS)
TMS = 256
NT_S = -(-N_TOK // TMS) + N_CLS
LANES = 128
ROUTER_LANES = 128
VMEM_LIMIT = 56 * 1024 * 1024


def _cparams(sem):
    return pltpu.CompilerParams(dimension_semantics=sem, vmem_limit_bytes=VMEM_LIMIT)


def _mod_body(c_ref, w_ref, b_ref, o_ref):
    c = c_ref[...]
    a = (c * jax.nn.sigmoid(c)).astype(bf16)
    o_ref[...] = jnp.dot(a, w_ref[...], preferred_element_type=f32) + b_ref[...]


def _modulation(c_all, w_ada, b_ada):
    R = c_all.shape[0]
    return pl.pallas_call(
        _mod_body,
        grid=(L, 6),
        in_specs=[pl.BlockSpec((R, D), lambda l, j: (0, 0)),
                  pl.BlockSpec((None, D, D), lambda l, j: (l, 0, j)),
                  pl.BlockSpec((None, 1, D), lambda l, j: (l, 0, j))],
        out_specs=pl.BlockSpec((None, R, D), lambda l, j: (l, 0, j)),
        out_shape=jax.ShapeDtypeStruct((L, R, 6 * D), f32),
        compiler_params=_cparams(("arbitrary", "arbitrary")),
        name="modulation",
    )(c_all, w_ada, b_ada)


def _mod_specs(l, chunk, tm):
    nt_p, tiles_per_batch = NP_TOK // tm, T // tm
    p = pl.BlockSpec((None, None, 1, D), lambda i, *_: (l, jnp.minimum(i // tiles_per_batch, B - 1), 0, chunk))
    s = pl.BlockSpec((None, tm, D), lambda i, *_: (l, jnp.maximum(i - nt_p, 0), chunk))
    return p, s


def _pick_mod(i, nt_p, mp_ref, ms_ref):
    return jnp.where(i >= nt_p, ms_ref[...], mp_ref[...])


def _swap_halves(x):
    lane = lax.broadcasted_iota(jnp.int32, x.shape, 1)
    return jnp.where((lane % HD) < (HD // 2), pltpu.roll(x, LANES - HD // 2, axis=1), pltpu.roll(x, HD // 2, axis=1))


def _modproj_body(x_ref, shp_ref, shs_ref, scp_ref, scs_ref, cos_ref, sin_ref, w_ref,
                  qa_ref, ka_ref, va_ref, qb_ref, kbd_ref, vbd_ref, kb_ref, vb_ref):
    i = pl.program_id(0)
    sh = _pick_mod(i, NT_P, shp_ref, shs_ref)
    sc = _pick_mod(i, NT_P, scp_ref, scs_ref)
    h = (x_ref[...] * (1.0 + sc) + sh).astype(bf16)
    r = jnp.dot(h, w_ref[...], preferred_element_type=f32)
    cos = cos_ref[...]
    sin = sin_ref[...]

    def rope(c0, width):
        parts = []
        for c in range(c0, c0 + width, LANES):
            x = r[:, c:c + LANES]
            parts.append(x * cos + _swap_halves(x) * sin)
        return parts

    qa = rope(0, D_QA)
    ka = rope(D_QA, D_QA)
    qb = rope(3 * D_QA, D_QA)
    kb = rope(4 * D_QA, LANES)[0]
    vb = r[:, 4 * D_QA + LANES:4 * D_QA + 2 * LANES]
    qa_ref[...] = jnp.concatenate(qa, axis=1)
    ka_ref[...] = jnp.concatenate(ka, axis=1)
    va_ref[...] = r[:, 2 * D_QA:3 * D_QA]
    qb_ref[...] = jnp.concatenate(qb, axis=1).astype(bf16)
    kb_ref[...] = kb
    vb_ref[...] = vb

    def dup(x):
        lane = lax.broadcasted_iota(jnp.int32, x.shape, 1)
        rolled = pltpu.roll(x, HD, axis=1)
        lo = jnp.where(lane < HD, x, rolled)
        hi = jnp.where(lane < HD, rolled, x)
        return jnp.concatenate([lo, hi], axis=1).astype(bf16)

    kbd_ref[...] = dup(kb)
    vbd_ref[...] = dup(vb)


def _modproj(l, x_all, modp, mods, cos_t, sin_t, w_in):
    shp, shs = _mod_specs(l, 0, TM)
    scp, scs = _mod_specs(l, 1, TM)
    nrope = T // TM

    def rope_idx(i):
        return (jnp.where(i < NT_P, i % nrope, nrope + i - NT_P), 0)

    tok = lambda w: pl.BlockSpec((TM, w), lambda i: (i, 0))
    sds = lambda w, dt: jax.ShapeDtypeStruct((N_TOK, w), dt)
    return pl.pallas_call(
        _modproj_body,
        grid=(NT,),
        in_specs=[tok(D), shp, shs, scp, scs,
                  pl.BlockSpec((TM, LANES), rope_idx), pl.BlockSpec((TM, LANES), rope_idx),
                  pl.BlockSpec((None, D, D_IN), lambda i: (l, 0, 0))],
        out_specs=[tok(D_QA), tok(D_QA), tok(D_QA), tok(D_QA), tok(2 * LANES), tok(2 * LANES), tok(LANES), tok(LANES)],
        out_shape=[sds(D_QA, f32), sds(D_QA, f32), sds(D_QA, f32), sds(D_QA, bf16),
                   sds(2 * LANES, bf16), sds(2 * LANES, bf16), sds(LANES, f32), sds(LANES, f32)],
        compiler_params=_cparams(("arbitrary",)),
        name="modproj",
    )(x_all, modp, mods, modp, mods, cos_t, sin_t, w_in)


def _pair_attention(q, k, v, off, sink_col=None):
    lane = lax.broadcasted_iota(jnp.int32, (BLK, LANES), 1)
    q = q * SCALE
    zero = jnp.zeros_like(q)
    qs = jnp.concatenate([jnp.where(lane < HD, q, zero), jnp.where(lane >= HD, q, zero)], axis=0).astype(bf16)
    s = lax.dot_general(qs, k, (((1,), (1,)), ((), ())), preferred_element_type=f32)
    row = lax.broadcasted_iota(jnp.int32, (2 * BLK, 2 * BLK), 0) % BLK
    col = lax.broadcasted_iota(jnp.int32, (2 * BLK, 2 * BLK), 1)
    dist = lax.bitcast_convert_type(off + (row - col), jnp.uint32)
    max_dist = BLK if sink_col is None else WIN_B - 1
    s = jnp.where(dist <= max_dist, s, NEG)
    m = jnp.max(s, axis=1, keepdims=True)
    if sink_col is not None:
        m = jnp.maximum(m, sink_col)
    p = jnp.exp(s - m)
    den = jnp.sum(p, axis=1, keepdims=True)
    if sink_col is not None:
        den = den + jnp.exp(sink_col - m)
    o = jnp.dot(p.astype(bf16), v, preferred_element_type=f32) / den
    lse = jnp.broadcast_to(m + jnp.log(den), (2 * BLK, LANES))
    o_pair = jnp.where(lane < HD, o[:BLK], o[BLK:])
    lse_pair = jnp.where(lane < HD, lse[:BLK], lse[BLK:])
    return o_pair, lse_pair


def _attn_a_p_body(q_ref, k_ref, v_ref, o_ref, acc_ref, m_ref, s_ref):
    for pi, (window, dil) in enumerate(PATTERNS):
        nblk = T // dil // BLK
        first = pi == 0

        def blocks(it, _, dil=dil, nblk=nblk, first=first):
            work = []
            for u in range(UNROLL):
                idx = it * UNROLL + u
                c = idx // nblk
                j = idx % nblk
                kj = jnp.maximum(j - 1, 0)
                off = (j - kj) * BLK
                if dil == 1:
                    qrows = pl.ds(pl.multiple_of(j * BLK, BLK), BLK)
                    krows = pl.ds(pl.multiple_of(kj * BLK, BLK), 2 * BLK)
                else:
                    qrows = pl.ds(c + dil * BLK * j, BLK, stride=dil)
                    krows = pl.ds(c + dil * BLK * kj, 2 * BLK, stride=dil)
                o, lse = _pair_attention(q_ref[qrows, :], k_ref[krows, :].astype(bf16), v_ref[krows, :].astype(bf16), off)
                if first:
                    work.append((qrows, o, lse, jnp.ones_like(lse)))
                else:
                    m_old = m_ref[qrows, :]
                    m_new = jnp.maximum(m_old, lse)
                    a = jnp.exp(m_old - m_new)
                    b = jnp.exp(lse - m_new)
                    work.append((qrows, acc_ref[qrows, :] * a + o * b, m_new, s_ref[qrows, :] * a + b))
            for qrows, acc, m_new, s_new in work:
                acc_ref[qrows, :] = acc
                m_ref[qrows, :] = m_new
                s_ref[qrows, :] = s_new
            return 0

        lax.fori_loop(0, T // BLK // UNROLL, blocks, 0)
    o_ref[...] = (acc_ref[...] / s_ref[...]).astype(o_ref.dtype)


def _attn_a_prompt(qa, ka, va):
    spec = pl.BlockSpec((T, LANES), lambda b, p: (b, p))
    return pl.pallas_call(
        _attn_a_p_body,
        grid=(B, D_QA // LANES),
        in_specs=[spec, spec, spec],
        out_specs=spec,
        out_shape=jax.ShapeDtypeStruct((NP_TOK, D_QA), bf16),
        scratch_shapes=[pltpu.VMEM((T, LANES), f32)] * 3,
        compiler_params=_cparams(("arbitrary", "arbitrary")),
        name="attn_a_prompt",
    )(qa, ka, va)


def _attn_b_p_body(l, sink_ref, q_ref, k_ref, v_ref, o_ref):
    p = pl.program_id(1)
    rowi = lax.broadcasted_iota(jnp.int32, (2 * BLK, 1), 0)
    sink_col = jnp.where(rowi < BLK, sink_ref[l, 2 * p], sink_ref[l, 2 * p + 1])

    def blocks(it, _):
        work = []
        for u in range(UNROLL):
            j = it * UNROLL + u
            kj = jnp.maximum(j - 1, 0)
            off = (j - kj) * BLK
            qrows = pl.ds(pl.multiple_of(j * BLK, BLK), BLK)
            krows = pl.ds(pl.multiple_of(kj * BLK, BLK), 2 * BLK)
            o, _ = _pair_attention(q_ref[qrows, :], k_ref[krows, :], v_ref[krows, :], off, sink_col)
            work.append((qrows, o))
        for qrows, o in work:
            o_ref[qrows, :] = o.astype(o_ref.dtype)
        return 0

    lax.fori_loop(0, T // BLK // UNROLL, blocks, 0)


def _attn_b_prompt(l, sinks, qb, kbd, vbd):
    qspec = pl.BlockSpec((T, LANES), lambda b, p: (b, p))
    kspec = pl.BlockSpec((T, LANES), lambda b, p: (b, p // 2))
    return pl.pallas_call(
        functools.partial(_attn_b_p_body, l),
        grid=(B, D_QA // LANES),
        in_specs=[pl.BlockSpec(memory_space=pltpu.SMEM), qspec, kspec, kspec],
        out_specs=qspec,
        out_shape=jax.ShapeDtypeStruct((NP_TOK, D_QA), bf16),
        compiler_params=_cparams(("arbitrary", "arbitrary")),
        name="attn_b_prompt",
    )(sinks, qb, kbd, vbd)


TC = 256


def _cache_t_body(k_ref, v_ref, ok_ref, ov_ref):
    ok_ref[...] = k_ref[...].T
    ov_ref[...] = v_ref[...].T


def _cache_t(ka, va):
    nb = W_A // TC
    ispec = pl.BlockSpec((TC, D_QA), lambda b, j: (b * (T // TC) + (T - W_A) // TC + j, 0))
    ospec = pl.BlockSpec((None, D_QA, TC), lambda b, j: (b, 0, j))
    sds = jax.ShapeDtypeStruct((B, D_QA, W_A), f32)
    return pl.pallas_call(
        _cache_t_body,
        grid=(B, nb),
        in_specs=[ispec, ispec],
        out_specs=[ospec, ospec],
        out_shape=[sds, sds],
        compiler_params=_cparams(("arbitrary", "arbitrary")),
        name="cache_t",
    )(ka, va)


_A_RANGES = tuple(W_A - max(window, 2 * LANES) for window, _ in PATTERNS)


def _place_new_rows(new_h):
    col = lax.broadcasted_iota(jnp.int32, (S, LANES), 1)
    row = lax.broadcasted_iota(jnp.int32, (S, LANES), 0)
    sel = jnp.where(col == LANES - S + row, 1.0, 0.0).astype(bf16)
    hi = new_h.astype(bf16)
    r1 = new_h - hi.astype(f32)
    mid = r1.astype(bf16)
    lo = (r1 - mid.astype(f32)).astype(bf16)
    dn = (((0,), (0,)), ((), ()))
    return (lax.dot_general(hi, sel, dn, preferred_element_type=f32)
            + lax.dot_general(mid, sel, dn, preferred_element_type=f32)
            + lax.dot_general(lo, sel, dn, preferred_element_type=f32))


def _attn_a_s_body(q_ref, kn_ref, vn_ref, kc_ref, vc_ref, o_ref, ok_ref, ov_ref):
    q = q_ref[...]
    kn = kn_ref[...]
    vn = vn_ref[...]
    srow = lax.broadcasted_iota(jnp.int32, (S, W_A), 0)
    jcol = lax.broadcasted_iota(jnp.int32, (S, W_A), 1)
    delta_c = W_A + srow - jcol
    sr = lax.broadcasted_iota(jnp.int32, (S, S), 0)
    jc = lax.broadcasted_iota(jnp.int32, (S, S), 1)
    delta_n = sr - jc
    masks_c = [((delta_c & (dil - 1)) == 0) & (delta_c <= window) for window, dil in PATTERNS]
    masks_n = [((delta_n & (dil - 1)) == 0) & (delta_n >= 0) & (delta_n <= window) for window, dil in PATTERNS]
    lane_last = lax.broadcasted_iota(jnp.int32, (HD, LANES), 1)
    outs = []
    for h in range(H_A):
        qh = q[:, h * HD:(h + 1) * HD].astype(bf16)
        knf = kn[:, h * HD:(h + 1) * HD]
        vnf = vn[:, h * HD:(h + 1) * HD]
        knh = knf.astype(bf16)
        vnh = vnf.astype(bf16)
        kch = kc_ref[h]
        vch = vc_ref[h]
        kcb = kch.astype(bf16)
        vcb = vch.astype(bf16)
        s_c = jnp.dot(qh, kcb, preferred_element_type=f32) * SCALE
        s_n = lax.dot_general(qh, knh, (((1,), (1,)), ((), ())), preferred_element_type=f32) * SCALE
        acc = m_run = s_run = None
        for pi in range(len(PATTERNS)):
            c0 = _A_RANGES[pi]
            sc = jnp.where(masks_c[pi][:, c0:], s_c[:, c0:], NEG)
            sn = jnp.where(masks_n[pi], s_n, NEG)
            m = jnp.maximum(jnp.max(sc, axis=1, keepdims=True), jnp.max(sn, axis=1, keepdims=True))
            pc = jnp.exp(sc - m)
            pn = jnp.exp(sn - m)
            den = jnp.sum(pc, axis=1, keepdims=True) + jnp.sum(pn, axis=1, keepdims=True)
            o = (lax.dot_general(pc.astype(bf16), vcb[:, c0:], (((1,), (1,)), ((), ())), preferred_element_type=f32)
                 + jnp.dot(pn.astype(bf16), vnh, preferred_element_type=f32)) / den
            lse = m + jnp.log(den)
            if acc is None:
                acc, m_run, s_run = o, lse, jnp.ones_like(lse)
            else:
                m_new = jnp.maximum(m_run, lse)
                a = jnp.exp(m_run - m_new)
                b = jnp.exp(lse - m_new)
                acc = acc * a + o * b
                s_run = s_run * a + b
                m_run = m_new
        outs.append(acc / s_run)
        for cache_h, new_h, dst in ((kch, knf, ok_ref), (vch, vnf, ov_ref)):
            rolled = pltpu.roll(cache_h, W_A - S, axis=1)
            dst[h, :, :W_A - LANES] = rolled[:, :W_A - LANES]
            dst[h, :, W_A - LANES:] = jnp.where(lane_last >= LANES - S, _place_new_rows(new_h), rolled[:, W_A - LANES:])
    o_ref[...] = jnp.concatenate(outs, axis=1).astype(o_ref.dtype)


def _attn_a_sample(l, qa, ka, va, cak_t, cav_t, prev):
    row0 = NP_TOK // S
    tok = pl.BlockSpec((S, D_QA), lambda n: (row0 + n, 0))
    cspec = pl.BlockSpec((None, None, H_A, HD, W_A), lambda n: (l, n, 0, 0, 0))
    big = jax.ShapeDtypeStruct((L, NS, H_A, HD, W_A), f32)
    args = [qa, ka, va, cak_t, cav_t]
    in_specs = [tok, tok, tok, cspec, cspec]
    aliases = {}
    body = _attn_a_s_body
    if prev is not None:
        args += list(prev)
        in_specs += [pl.BlockSpec(memory_space=pl.ANY)] * 2
        aliases = {5: 1, 6: 2}
        body = lambda *refs: _attn_a_s_body(*refs[:5], *refs[7:])
    return pl.pallas_call(
        body,
        grid=(NS,),
        in_specs=in_specs,
        out_specs=[pl.BlockSpec((S, D_QA), lambda n: (n, 0)), cspec, cspec],
        out_shape=[jax.ShapeDtypeStruct((NS_TOK, D_QA), f32), big, big],
        input_output_aliases=aliases,
        compiler_params=_cparams(("arbitrary",)),
        name="attn_a_sample",
    )(*args)


NB_S = 16
KB_ALL = 136


def _attn_b_s_body(l, sink_ref, qb_ref, k_ref, v_ref, o_ref):
    lane = lax.broadcasted_iota(jnp.int32, (NB_S, KB_ALL, LANES), 2)
    lane_q = lax.broadcasted_iota(jnp.int32, (NB_S, S, LANES), 2)
    row = lax.broadcasted_iota(jnp.int32, (2 * S, KB_ALL), 0) % S
    col = lax.broadcasted_iota(jnp.int32, (2 * S, KB_ALL), 1)
    dist = W_B + row - col
    valid = ((dist >= 0) & (dist < WIN_B) & (col < W_B + S))[None]
    rowi = lax.broadcasted_iota(jnp.int32, (1, 2 * S, 1), 1)
    k = k_ref[...]
    v = v_ref[...]
    kr = pltpu.roll(k, HD, axis=2)
    vr = pltpu.roll(v, HD, axis=2)
    q_all = qb_ref[...].astype(f32)
    kd, vd = [], []
    for g in range(KV_B):
        kd.append((jnp.where(lane < HD, k, kr) if g == 0 else jnp.where(lane < HD, kr, k)).astype(bf16))
        vd.append((jnp.where(lane < HD, v, vr) if g == 0 else jnp.where(lane < HD, vr, v)).astype(bf16))
    outs = []
    for p in range(D_QA // LANES):
        g = p // 2
        q = q_all[:, p * LANES:(p + 1) * LANES].reshape(NB_S, S, LANES)
        zero = jnp.zeros_like(q)
        qs = jnp.concatenate([jnp.where(lane_q < HD, q, zero), jnp.where(lane_q >= HD, q, zero)], axis=1).astype(bf16)
        s = jnp.einsum("nqd,nkd->nqk", qs, kd[g], preferred_element_type=f32) * SCALE
        s = jnp.where(valid, s, NEG)
        sink_col = jnp.where(rowi < S, sink_ref[l, 2 * p], sink_ref[l, 2 * p + 1])
        m = jnp.maximum(jnp.max(s, axis=2, keepdims=True), sink_col)
        pr = jnp.exp(s - m)
        den = jnp.sum(pr, axis=2, keepdims=True) + jnp.exp(sink_col - m)
        o = jnp.einsum("nqk,nkd->nqd", pr.astype(bf16), vd[g], preferred_element_type=f32) / den
        outs.append(jnp.where(lane_q < HD, o[:, :S], o[:, S:]).reshape(NB_S * S, LANES))
    o_ref[...] = jnp.concatenate(outs, axis=1).astype(o_ref.dtype)


def _attn_b_sample(l, sinks, qb, kall, vall):
    blk0 = NP_TOK // (NB_S * S)
    cspec = pl.BlockSpec((NB_S, KB_ALL, LANES), lambda i: (i, 0, 0))
    return pl.pallas_call(
        functools.partial(_attn_b_s_body, l),
        grid=(NS // NB_S,),
        in_specs=[pl.BlockSpec(memory_space=pltpu.SMEM),
                  pl.BlockSpec((NB_S * S, D_QA), lambda i: (blk0 + i, 0)), cspec, cspec],
        out_specs=pl.BlockSpec((NB_S * S, D_QA), lambda i: (i, 0)),
        out_shape=jax.ShapeDtypeStruct((NS_TOK, D_QA), f32),
        compiler_params=_cparams(("arbitrary",)),
        name="attn_b_sample",
    )(sinks, qb, kall, vall)


def _layer_norm(x, g, b):
    mu = jnp.mean(x, axis=-1, keepdims=True)
    xc = x - mu
    var = jnp.mean(xc * xc, axis=-1, keepdims=True)
    return xc * lax.rsqrt(var + EPS) * g + b


def _split_bf16(x):
    hi = x.astype(bf16)
    return hi, (x - hi.astype(f32)).astype(bf16)


def _outproj_body(x_ref, oap_ref, oas_ref, obp_ref, obs_ref, g1p_ref, g1s_ref, sh2p_ref, sh2s_ref, sc2p_ref, sc2s_ref,
                  w_ref, lng_ref, lnb_ref, wr_ref, br_ref, x1_ref, hx_ref):
    i = pl.program_id(0)
    smp = i >= NT_P
    oa = jnp.where(smp, oas_ref[...].astype(bf16), oap_ref[...])
    ob = jnp.where(smp, obs_ref[...].astype(bf16), obp_ref[...])
    y = jnp.dot(jnp.concatenate([oa, ob], axis=1), w_ref[...], preferred_element_type=f32)
    g1 = _pick_mod(i, NT_P, g1p_ref, g1s_ref)
    x1 = _layer_norm(ALPHA * x_ref[...] + g1 * y, lng_ref[...], lnb_ref[...])
    x1_ref[...] = x1
    h2 = x1 * (1.0 + _pick_mod(i, NT_P, sc2p_ref, sc2s_ref)) + _pick_mod(i, NT_P, sh2p_ref, sh2s_ref)
    h_hi, h_lo = _split_bf16(h2)
    w_hi, w_lo = _split_bf16(wr_ref[...])
    logit = (jnp.dot(h_hi, w_hi, preferred_element_type=f32) + jnp.dot(h_hi, w_lo, preferred_element_type=f32)
             + jnp.dot(h_lo, w_hi, preferred_element_type=f32)) + br_ref[...]
    lt = logit.T
    lg = lt[0:N_GROUPS]
    le = lt[N_GROUPS:N_GROUPS + N_EXP]
    gi = lax.broadcasted_iota(jnp.int32, lg.shape, 0)
    ei = lax.broadcasted_iota(jnp.int32, le.shape, 0)
    big = jnp.int32(1 << 20)
    gmax = jnp.max(lg, axis=0, keepdims=True)
    g_sel = jnp.min(jnp.where(lg == gmax, gi, big), axis=0, keepdims=True)
    p_group = 1.0 / jnp.sum(jnp.exp(lg - gmax), axis=0, keepdims=True)
    in_g = (ei // E_PER_G) == g_sel
    lem = jnp.where(in_g, le, -jnp.inf)
    v1 = jnp.max(lem, axis=0, keepdims=True)
    i1 = jnp.min(jnp.where(in_g & (lem == v1), ei, big), axis=0, keepdims=True)
    lem2 = jnp.where(ei == i1, -jnp.inf, lem)
    v2 = jnp.max(lem2, axis=0, keepdims=True)
    i2 = jnp.min(jnp.where(in_g & (lem2 == v2), ei, big), axis=0, keepdims=True)
    e2 = jnp.exp(v2 - v1)
    w1 = (1.0 / (1.0 + e2)) * p_group
    w2 = (e2 / (1.0 + e2)) * p_group
    a1 = i1 - g_sel * E_PER_G
    a2 = i2 - g_sel * E_PER_G
    code = jnp.minimum(a1, a2) * E_PER_G + jnp.maximum(a1, a2)
    pair = jnp.zeros_like(code)
    slot_a = jnp.zeros_like(code)
    for k, (pa, pb) in enumerate(PAIRS):
        hit = code == min(pa, pb) * E_PER_G + max(pa, pb)
        pair = jnp.where(hit, k, pair)
        slot_a = jnp.where(hit, pa, slot_a)
    w_a = jnp.where(a1 == slot_a, w1, w2)
    w_b = jnp.where(a1 == slot_a, w2, w1)
    cls = (g_sel * len(PAIRS) + pair).astype(f32)
    ri = lax.broadcasted_iota(jnp.int32, (LANES, TM), 0)
    tail_t = jnp.where(ri == 0, w_a, 0.0) + jnp.where(ri == 1, w_b, 0.0) + jnp.where(ri == 2, cls, 0.0)
    hx_ref[...] = jnp.concatenate([h2, tail_t.T], axis=1)


def _outproj(l, x_all, oa_p, oa_s, ob_p, ob_s, modp, mods, w_out, ln_g, ln_b, w_r, b_r):
    tok = lambda w: pl.BlockSpec((TM, w), lambda i: (i, 0))
    tok_p = pl.BlockSpec((TM, D_QA), lambda i: (jnp.minimum(i, NT_P - 1), 0))
    tok_s = pl.BlockSpec((TM, D_QA), lambda i: (jnp.maximum(i - NT_P, 0), 0))
    g1 = _mod_specs(l, 2, TM)
    sh2 = _mod_specs(l, 3, TM)
    sc2 = _mod_specs(l, 4, TM)
    full = lambda r, c: pl.BlockSpec((None, r, c), lambda i: (l, 0, 0))
    return pl.pallas_call(
        _outproj_body,
        grid=(NT,),
        in_specs=[tok(D), tok_p, tok_s, tok_p, tok_s, *g1, *sh2, *sc2,
                  full(D, D), full(1, D), full(1, D), full(D, ROUTER_LANES), full(1, ROUTER_LANES)],
        out_specs=[tok(D), tok(D_EXT)],
        out_shape=[jax.ShapeDtypeStruct((N_TOK, D), f32), jax.ShapeDtypeStruct((N_TOK, D_EXT), f32)],
        compiler_params=_cparams(("arbitrary",)),
        name="outproj",
    )(x_all, oa_p, oa_s, ob_p, ob_s, modp, mods, modp, mods, modp, mods, w_out, ln_g, ln_b, w_r, b_r)


def _route(hx):
    cls = hx[:, D + 2].astype(jnp.int32)
    order = jnp.argsort(cls, stable=True).astype(jnp.int32)
    cid = jnp.arange(N_CLS, dtype=jnp.int32)
    counts = jnp.sum((cls[:, None] == cid[None, :]).astype(jnp.int32), axis=0)
    ntile = (counts + TMS - 1) // TMS
    tile_end = jnp.cumsum(ntile)
    n_used = tile_end[-1]
    t = jnp.arange(NT_S, dtype=jnp.int32)
    tile_cls = jnp.sum((jnp.minimum(t, n_used - 1)[:, None] >= tile_end[None, :]).astype(jnp.int32), axis=1)
    hot = (tile_cls[:, None] == cid[None, :]).astype(jnp.int32)
    pick = lambda v: jnp.sum(hot * v[None, :], axis=1)
    k_in_cls = (t - pick(tile_end - ntile)) * TMS
    start = pick(jnp.cumsum(counts) - counts) + k_in_cls
    nvalid = jnp.where(t < n_used, jnp.clip(pick(counts) - k_in_cls, 0, TMS), 0)
    r = jnp.arange(TMS, dtype=jnp.int32)[None, :]
    tok = order[jnp.clip(start[:, None] + r, 0, N_TOK - 1)]
    valid = r < nvalid[:, None]
    src_g = jnp.where(valid, tok, 0).reshape(NT_S, 1, TMS)
    src_s = jnp.where(valid, tok, N_TOK + t[:, None] * TMS + r).reshape(NT_S, 1, TMS)
    pa = jnp.asarray([p[0] for p in PAIRS], jnp.int32)
    pb = jnp.asarray([p[1] for p in PAIRS], jnp.int32)
    grp = tile_cls // len(PAIRS)
    pair_hot = ((tile_cls % len(PAIRS))[:, None] == jnp.arange(len(PAIRS), dtype=jnp.int32)[None, :]).astype(jnp.int32)
    e_a = grp * E_PER_G + jnp.sum(pair_hot * pa[None, :], axis=1)
    e_b = grp * E_PER_G + jnp.sum(pair_hot * pb[None, :], axis=1)
    i32 = lambda v: v.astype(jnp.int32)
    return i32(src_g), i32(src_s), i32(e_a), i32(e_b), i32(n_used.reshape(1))


def _row_copy_in(hx_hbm, idx_ref, buf, sem, r):
    return pltpu.make_async_copy(hx_hbm.at[pl.ds(idx_ref[0, r], 1), :], buf.at[pl.ds(r, 1), :], sem)


def _row_copy_out(buf, y_hbm, idx_ref, sem, r):
    return pltpu.make_async_copy(buf.at[pl.ds(r, 1), :], y_hbm.at[pl.ds(idx_ref[0, r], 1), :], sem)


def _experts_body(ea_ref, eb_ref, nu_ref, g_cur_ref, g_nxt_ref, s_cur_ref, s_prv_ref, s_pp_ref, hx_hbm,
                  wga_ref, wua_ref, wda_ref, wgb_ref, wub_ref, wdb_ref, y_hbm, xbuf, obuf, gsem, ssem):
    t = pl.program_id(0)
    n_used = nu_ref[0]
    slot = t % 2

    def gather(idx_ref, sl, go):
        def one(r, _):
            cp = _row_copy_in(hx_hbm, idx_ref, xbuf.at[sl], gsem.at[sl], r)
            cp.start() if go else cp.wait()
            return 0
        lax.fori_loop(0, TMS, one, 0, unroll=8)

    def scatter(idx_ref, sl, go):
        def one(r, _):
            cp = _row_copy_out(obuf.at[sl], y_hbm, idx_ref, ssem.at[sl], r)
            cp.start() if go else cp.wait()
            return 0
        lax.fori_loop(0, TMS, one, 0, unroll=8)

    @pl.when((t == 0) & (n_used > 0))
    def _():
        gather(g_cur_ref, 0, True)

    @pl.when(t + 1 < n_used)
    def _():
        gather(g_nxt_ref, 1 - slot, True)

    @pl.when((t >= 2) & (t - 2 < n_used))
    def _():
        scatter(s_pp_ref, slot, False)

    @pl.when(t < n_used)
    def _():
        gather(g_cur_ref, slot, False)
        x = xbuf[slot]
        xb = x[:, :D].astype(bf16)
        out = None
        for wcol, wg_ref, wu_ref, wd_ref in ((0, wga_ref, wua_ref, wda_ref), (1, wgb_ref, wub_ref, wdb_ref)):
            gt = jnp.dot(xb, wg_ref[...], preferred_element_type=f32)
            up = jnp.dot(xb, wu_ref[...], preferred_element_type=f32)
            a = ((gt * jax.nn.sigmoid(gt)) * up).astype(bf16)
            yo = x[:, D + wcol:D + wcol + 1] * jnp.dot(a, wd_ref[...], preferred_element_type=f32)
            out = yo if out is None else out + yo
        obuf[slot] = out
        scatter(s_cur_ref, slot, True)

    @pl.when(t == NT_S - 1)
    def _():
        @pl.when((t >= 1) & (t - 1 < n_used))
        def _():
            scatter(s_prv_ref, 1 - slot, False)

        @pl.when(t < n_used)
        def _():
            scatter(s_cur_ref, slot, False)


def _experts(l, hx, route, wg, wu, wd):
    src_g, src_s, e_a, e_b, n_used = route
    smem = lambda f: pl.BlockSpec((None, 1, TMS), f, memory_space=pltpu.SMEM)
    last = NT_S - 1
    wspec = lambda e_idx, r, c: pl.BlockSpec((None, None, r, c), lambda t, ea, eb, nu: (l, (ea, eb)[e_idx][t], 0, 0))
    grid_spec = pltpu.PrefetchScalarGridSpec(
        num_scalar_prefetch=3,
        grid=(NT_S,),
        in_specs=[smem(lambda t, *_: (t, 0, 0)), smem(lambda t, *_: (jnp.minimum(t + 1, last), 0, 0)),
                  smem(lambda t, *_: (t, 0, 0)), smem(lambda t, *_: (jnp.maximum(t - 1, 0), 0, 0)),
                  smem(lambda t, *_: (jnp.maximum(t - 2, 0), 0, 0)),
                  pl.BlockSpec(memory_space=pl.ANY),
                  wspec(0, D, D_EXP), wspec(0, D, D_EXP), wspec(0, D_EXP, D),
                  wspec(1, D, D_EXP), wspec(1, D, D_EXP), wspec(1, D_EXP, D)],
        out_specs=pl.BlockSpec(memory_space=pl.ANY),
        scratch_shapes=[pltpu.VMEM((2, TMS, D_EXT), f32), pltpu.VMEM((2, TMS, D), f32),
                        pltpu.SemaphoreType.DMA((2,)), pltpu.SemaphoreType.DMA((2,))],
    )
    return pl.pallas_call(
        _experts_body,
        grid_spec=grid_spec,
        out_shape=jax.ShapeDtypeStruct((N_TOK + NT_S * TMS, D), f32),
        compiler_params=_cparams(("arbitrary",)),
        name="experts",
    )(e_a, e_b, n_used, src_g, src_g, src_s, src_s, src_s, hx, wg, wu, wd, wg, wu, wd)


def _ln2_body(x1_ref, y_ref, g2p_ref, g2s_ref, lng_ref, lnb_ref, o_ref):
    i = pl.program_id(0)
    g2 = _pick_mod(i, NT_P, g2p_ref, g2s_ref)
    o_ref[...] = _layer_norm(ALPHA * x1_ref[...] + g2 * y_ref[...], lng_ref[...], lnb_ref[...])


def _ln2(l, x1, y, modp, mods, ln_g, ln_b):
    tok = pl.BlockSpec((TM, D), lambda i: (i, 0))
    g2 = _mod_specs(l, 5, TM)
    full = pl.BlockSpec((None, 1, D), lambda i: (l, 0, 0))
    return pl.pallas_call(
        _ln2_body,
        grid=(NT,),
        in_specs=[tok, tok, *g2, full, full],
        out_specs=tok,
        out_shape=jax.ShapeDtypeStruct((N_TOK, D), f32),
        compiler_params=_cparams(("arbitrary",)),
        name="ln2",
    )(x1, y, modp, mods, ln_g, ln_b)


def _rope_tables():
    half = HD // 2
    inv_freq = THETA ** (-jnp.arange(half, dtype=f32) / half)
    pos = jnp.concatenate([jnp.arange(T), jnp.tile(PAST + jnp.arange(S), NS)]).astype(f32)
    ang = pos[:, None] * inv_freq[None, :]
    cos, sin = jnp.cos(ang), jnp.sin(ang)
    cos_t = jnp.tile(jnp.concatenate([cos, cos], axis=1), (1, LANES // HD))
    sin_t = jnp.tile(jnp.concatenate([-sin, sin], axis=1), (1, LANES // HD))
    return cos_t, sin_t


def kernel(x_prompt, x_sample, cache_a_k, cache_a_v, cache_b_k, cache_b_v, c_prompt, c_sample, w_in, w_out, attn_sinks, w_ada, b_ada, ln1_g, ln1_b, ln2_g, ln2_b, w_router_group, b_router_group, w_router_expert, b_router_expert, w_gate, w_up, w_down):
    assert x_prompt.shape == (B, T, D) and x_sample.shape == (NS, S, D)
    assert cache_a_k.shape == (L, NS, W_A, H_A, HD) and cache_b_k.shape == (L, NS, W_B, KV_B, HD)
    x_all = jnp.concatenate([x_prompt.reshape(NP_TOK, D), x_sample.reshape(NS_TOK, D)], axis=0)
    c_all = jnp.concatenate([jnp.repeat(c_sample, S, axis=0), c_prompt, jnp.zeros((4, D), f32)], axis=0)
    mod = _modulation(c_all, w_ada.astype(bf16), b_ada.reshape(L, 1, 6 * D))
    cos_t, sin_t = _rope_tables()
    w_in_b, w_out_b = w_in.astype(bf16), w_out.astype(bf16)
    wg_b, wu_b, wd_b = w_gate.astype(bf16), w_up.astype(bf16), w_down.astype(bf16)
    pad = ROUTER_LANES - N_GROUPS - N_EXP
    w_r = jnp.concatenate([w_router_group, w_router_expert, jnp.zeros((L, D, pad), f32)], axis=2)
    b_r = jnp.concatenate([b_router_group, b_router_expert, jnp.zeros((L, pad), f32)], axis=1).reshape(L, 1, ROUTER_LANES)
    modp = mod[:, NS_TOK:NS_TOK + B].reshape(L, B, 1, 6 * D)
    ln1g, ln1b, ln2g, ln2b = (a.reshape(L, 1, D) for a in (ln1_g, ln1_b, ln2_g, ln2_b))
    cak_t = jnp.transpose(cache_a_k, (0, 1, 3, 4, 2))
    cav_t = jnp.transpose(cache_a_v, (0, 1, 3, 4, 2))

    pa_k, pa_v, pb_k, pb_v, sb_k, sb_v = [], [], [], [], [], []
    sa = None
    for l in range(L):
        qa, ka, va, qb, kbd, vbd, kb, vb = _modproj(l, x_all, modp, mod, cos_t, sin_t, w_in_b)
        oa_p = _attn_a_prompt(qa, ka, va)
        ob_p = _attn_b_prompt(l, attn_sinks, qb, kbd, vbd)
        oa_s, sak, sav = _attn_a_sample(l, qa, ka, va, cak_t, cav_t, sa)
        sa = (sak, sav)

        def with_new(cache_l, new):
            new = new[NP_TOK:].reshape(NS, S, LANES)
            return jnp.concatenate([cache_l.reshape(NS, W_B, LANES), new, jnp.zeros((NS, 1, LANES), f32)], axis=1)

        kall = with_new(cache_b_k[l], kb)
        vall = with_new(cache_b_v[l], vb)
        ob_s = _attn_b_sample(l, attn_sinks, qb, kall, vall)

        x1, hx = _outproj(l, x_all, oa_p, oa_s, ob_p, ob_s, modp, mod, w_out_b, ln1g, ln1b, w_r, b_r)
        y = _experts(l, hx, _route(hx), wg_b, wu_b, wd_b)
        x_all = _ln2(l, x1, y, modp, mod, ln2g, ln2b)

        pak_t, pav_t = _cache_t(ka, va)
        pa_k.append(pak_t.reshape(B, H_A, HD, W_A))
        pa_v.append(pav_t.reshape(B, H_A, HD, W_A))
        pb_k.append(kb[:NP_TOK].reshape(B, T, KV_B, HD)[:, T - W_B:])
        pb_v.append(vb[:NP_TOK].reshape(B, T, KV_B, HD)[:, T - W_B:])
        sb_k.append(kall[:, S:S + W_B].reshape(NS, W_B, KV_B, HD))
        sb_v.append(vall[:, S:S + W_B].reshape(NS, W_B, KV_B, HD))

    y_prompt = x_all[:NP_TOK].reshape(B, T, D)
    y_sample = x_all[NP_TOK:].reshape(NS, S, D)
    sa_k = jnp.transpose(sa[0], (0, 1, 4, 2, 3))
    sa_v = jnp.transpose(sa[1], (0, 1, 4, 2, 3))
    pa_k = jnp.transpose(jnp.stack(pa_k), (0, 1, 4, 2, 3))
    pa_v = jnp.transpose(jnp.stack(pa_v), (0, 1, 4, 2, 3))
    return (y_prompt, y_sample, pa_k, pa_v, jnp.stack(pb_k), jnp.stack(pb_v),
            sa_k, sa_v, jnp.stack(sb_k), jnp.stack(sb_v))
```
